```python
import math
import jax, jax.numpy as jnp
from jax import lax
import numpy as np

D_MODEL = 1024
BATCH = 8
SEQ = 2048
DEPTH = 1
DEC_BATCH = 128
DEC_SEQ = 8
PAST_LEN = 8192
PAGE_SIZE = 128

N_HEADS = 16
HEAD_DIM = 64
KV_HEADS = 2
GROUP = N_HEADS // KV_HEADS
CMP_BLOCK = 32
CMP_STRIDE = 16
CMP_HIDDEN = 2 * HEAD_DIM
SEL_BLOCK = 64
N_SELECT = 16
N_LOCAL = 2
WINDOW = 512
Q_BLOCK = 128
KV_SLOTS = 4
D_INNER = D_MODEL
SSM_HEAD_DIM = 64
SSM_HEADS = D_INNER // SSM_HEAD_DIM
SSM_GROUPS = 4
HEADS_PER_GROUP = SSM_HEADS // SSM_GROUPS
D_STATE = 128
CONV_W = 4
CONV_DIM = D_INNER + 2 * SSM_GROUPS * D_STATE
SSD_CHUNK = 128
N_EXPERT_GROUPS = 4
EXPERTS_PER_GROUP = 8
N_EXPERTS = N_EXPERT_GROUPS * EXPERTS_PER_GROUP
TOP_K_IN_GROUP = 2
D_EXPERT = 512
EPS = 1e-6
IN_SPLITS = (N_HEADS * HEAD_DIM, 6 * KV_HEADS * HEAD_DIM, 3 * N_HEADS, D_INNER, CONV_DIM, SSM_HEADS, 2 * D_MODEL)
D_IN_PROJ = N_HEADS * HEAD_DIM + 6 * KV_HEADS * HEAD_DIM + 3 * N_HEADS + D_INNER + CONV_DIM + SSM_HEADS + 2 * D_MODEL

kernel_name = 'hybrid_nsa_mamba2_hmoe_step'


def rmsnorm(x, g):
    xf = x.astype(jnp.float32)
    xf = xf * lax.rsqrt(jnp.mean(xf * xf, axis=-1, keepdims=True) + EPS)
    return (xf * g.astype(jnp.float32)).astype(x.dtype)


def masked_softmax(logits, mask):
    logits = jnp.where(mask, logits.astype(jnp.float32), -jnp.inf)
    m = jnp.max(logits, axis=-1, keepdims=True)
    p = jnp.exp(logits - jnp.where(jnp.isfinite(m), m, 0.0))
    return p / jnp.maximum(jnp.sum(p, axis=-1, keepdims=True), 1e-30)


def alibi_slopes():
    h = jnp.arange(1, N_HEADS + 1, dtype=jnp.float32)
    return jnp.exp2(-8.0 * h / N_HEADS).reshape(KV_HEADS, GROUP)


def mixer_inputs(x, norm1, w_in, q_norm, k_norm):
    b, t = x.shape[:2]
    xn = rmsnorm(x, norm1)
    points = [int(v) for v in np.cumsum(IN_SPLITS)[:-1]]
    q, kv, ag, z, xbc, dt_raw, mg = jnp.split(xn @ w_in, points, axis=-1)
    q = rmsnorm(q.reshape(b, t, KV_HEADS, GROUP, HEAD_DIM), q_norm) * (HEAD_DIM ** -0.5)
    kv = kv.reshape(b, t, 6, KV_HEADS, HEAD_DIM)
    sk = rmsnorm(kv[:, :, 2], k_norm[1])
    wk = rmsnorm(kv[:, :, 4], k_norm[2])
    rows = (kv[:, :, 0], kv[:, :, 1], sk, kv[:, :, 3])
    win = (wk, kv[:, :, 5])
    gate = jax.nn.sigmoid(ag.astype(jnp.float32)).astype(x.dtype).reshape(b, t, KV_HEADS, GROUP, 3)
    return q, gate, rows, win, z, xbc, dt_raw, mg


def compress(rows, pe, w1, w2):
    b, t = rows.shape[:2]
    nch = t // CMP_STRIDE
    ch = rows[:, :nch * CMP_STRIDE].reshape(b, nch, CMP_STRIDE, KV_HEADS, HEAD_DIM)
    ch = jnp.swapaxes(ch, 2, 3).reshape(b, nch, KV_HEADS, CMP_STRIDE * HEAD_DIM)
    half = CMP_STRIDE * HEAD_DIM
    lead = ch @ w1[:half]
    trail = ch @ w1[half:]
    h = lead[:, :-1] + trail[:, 1:] + pe.reshape(-1) @ w1
    return jax.nn.gelu(h) @ w2


def to_blocks(rows):
    b, t = rows.shape[:2]
    nb = -(-t // SEL_BLOCK)
    rows = jnp.pad(rows, ((0, 0), (0, nb * SEL_BLOCK - t), (0, 0), (0, 0)))
    return rows.reshape(b, nb, SEL_BLOCK, KV_HEADS, HEAD_DIM)


def nsa_context(ck, cv, sk, sv, k_norm, cmp_pe, cmp_w1, cmp_w2):
    kc = rmsnorm(compress(ck, cmp_pe[0], cmp_w1[0], cmp_w2[0]), k_norm[0])
    vc = compress(cv, cmp_pe[1], cmp_w1[1], cmp_w2[1])
    cend = jnp.arange(kc.shape[1], dtype=jnp.int32) * CMP_STRIDE + (CMP_BLOCK - 1)
    return kc, vc, cend, to_blocks(sk), to_blocks(sv)


def nsa_core(q, qpos, kc, vc, cend, kblk, vblk, kw, vw, wpos, gate, slopes):
    f32 = jnp.float32
    b, tq = q.shape[:2]
    nb = kblk.shape[1]
    slope5 = slopes[None, None, :, :, None]
    dist_c = qpos[:, None] - cend[None, :]
    s_c = jnp.einsum('btgrd,bngd->btgrn', q, kc).astype(f32) - slope5 * dist_c.astype(f32)[None, :, None, None, :]
    p_c = masked_softmax(s_c, (dist_c >= 0)[None, :, None, None, :])
    o_c = jnp.einsum('btgrn,bngd->btgrd', p_c.astype(vc.dtype), vc)
    jb = jnp.arange(nb, dtype=jnp.int32)
    cstart = cend - (CMP_BLOCK - 1)
    cover = ((cstart[:, None] < (jb[None, :] + 1) * SEL_BLOCK) & (cend[:, None] >= jb[None, :] * SEL_BLOCK)).astype(f32)
    imp = jnp.einsum('btgrn,nj->btgj', p_c, cover)
    back = (qpos // SEL_BLOCK)[:, None] - jb[None, :]
    forced = (jb[None, :] == 0) | ((back >= 0) & (back < N_LOCAL))
    imp = jnp.where(forced[None, :, None, :], jnp.inf, jnp.where((back < 0)[None, :, None, :], -jnp.inf, imp))
    n_top = min(N_SELECT, nb)
    _, idx = lax.top_k(imp, n_top)
    take = jax.vmap(jax.vmap(lambda blocks, ids: blocks[ids]))
    ix = jnp.moveaxis(idx, 2, 1).reshape(b, KV_HEADS, tq * n_top)
    ks = take(jnp.moveaxis(kblk, 3, 1), ix).reshape(b, KV_HEADS, tq, n_top, SEL_BLOCK, HEAD_DIM)
    vs = take(jnp.moveaxis(vblk, 3, 1), ix).reshape(b, KV_HEADS, tq, n_top, SEL_BLOCK, HEAD_DIM)
    spos = idx[..., None] * SEL_BLOCK + jnp.arange(SEL_BLOCK, dtype=jnp.int32)
    dist_s = (qpos[None, :, None, None, None] - spos)[:, :, :, None]
    s_s = jnp.einsum('btgrd,bgtksd->btgrks', q, ks).astype(f32) - slopes[None, None, :, :, None, None] * dist_s.astype(f32)
    p_s = masked_softmax(s_s.reshape(b, tq, KV_HEADS, GROUP, n_top * SEL_BLOCK),
                         (dist_s >= 0).reshape(b, tq, KV_HEADS, 1, n_top * SEL_BLOCK))
    o_s = jnp.einsum('btgrks,bgtksd->btgrd', p_s.reshape(b, tq, KV_HEADS, GROUP, n_top, SEL_BLOCK).astype(vs.dtype), vs)
    dist_w = qpos[:, None] - wpos[None, :]
    valid_w = (dist_w >= 0) & (dist_w < WINDOW) & (wpos[None, :] >= 0)
    s_w = jnp.einsum('btgrd,bsgd->btgrs', q, kw).astype(f32) - slope5 * dist_w.astype(f32)[None, :, None, None, :]
    p_w = masked_softmax(s_w, valid_w[None, :, None, None, :])
    o_w = jnp.einsum('btgrs,bsgd->btgrd', p_w.astype(vw.dtype), vw)
    return gate[..., 0:1] * o_c + gate[..., 1:2] * o_s + gate[..., 2:3] * o_w


def nsa_prompt(q, gate, rows, win, slopes, k_norm, cmp_pe, cmp_w1, cmp_w2):
    b, t = q.shape[:2]
    kc, vc, cend, kblk, vblk = nsa_context(rows[0], rows[1], rows[2], rows[3], k_norm, cmp_pe, cmp_w1, cmp_w2)
    pad = ((0, 0), (WINDOW, 0), (0, 0), (0, 0))
    wk = jnp.pad(win[0], pad)
    wv = jnp.pad(win[1], pad)
    nq = t // Q_BLOCK

    def one_block(args):
        qb, gb, start = args
        qpos = start + jnp.arange(Q_BLOCK, dtype=jnp.int32)
        wpos = start - WINDOW + jnp.arange(Q_BLOCK + WINDOW, dtype=jnp.int32)
        kwb = lax.dynamic_slice_in_dim(wk, start, Q_BLOCK + WINDOW, axis=1)
        vwb = lax.dynamic_slice_in_dim(wv, start, Q_BLOCK + WINDOW, axis=1)
        return nsa_core(qb, qpos, kc, vc, cend, kblk, vblk, kwb, vwb, wpos, gb, slopes)

    def blocks(a):
        return jnp.moveaxis(a.reshape((b, nq, Q_BLOCK) + a.shape[2:]), 1, 0)

    out = lax.map(one_block, (blocks(q), blocks(gate), jnp.arange(nq, dtype=jnp.int32) * Q_BLOCK))
    return jnp.moveaxis(out, 0, 1).reshape(b, t, N_HEADS * HEAD_DIM)


def nsa_sample(q, gate, rows_new, win_new, past_rows, win_buf, slopes, k_norm, cmp_pe, cmp_w1, cmp_w2):
    b, s = q.shape[:2]
    past_len = past_rows[0].shape[1]
    full = [jnp.concatenate([p, r], axis=1) for p, r in zip(past_rows, rows_new)]
    kc, vc, cend, kblk, vblk = nsa_context(full[0], full[1], full[2], full[3], k_norm, cmp_pe, cmp_w1, cmp_w2)
    w0 = win_buf.shape[1]
    wk = jnp.concatenate([win_buf[:, :, 0].astype(win_new[0].dtype), win_new[0]], axis=1)
    wv = jnp.concatenate([win_buf[:, :, 1].astype(win_new[1].dtype), win_new[1]], axis=1)
    wpos = past_len - w0 + jnp.arange(w0 + s, dtype=jnp.int32)
    qpos = past_len + jnp.arange(s, dtype=jnp.int32)
    out = nsa_core(q, qpos, kc, vc, cend, kblk, vblk, wk, wv, wpos, gate, slopes)
    keep = min(WINDOW, w0 + s)
    new_win = jnp.stack([wk[:, -keep:], wv[:, -keep:]], axis=2)
    return out.reshape(b, s, N_HEADS * HEAD_DIM), new_win


def ssd_scan(x, dt, a, bm, cm, h0):
    bsz, t = x.shape[:2]
    L = min(SSD_CHUNK, t)
    nc = -(-t // L)
    pad = nc * L - t

    def padt(u):
        return jnp.pad(u, [(0, 0), (0, pad)] + [(0, 0)] * (u.ndim - 2))

    x, dt, bm, cm = padt(x), padt(dt), padt(bm), padt(cm)
    xg = (x * dt[..., None]).reshape(bsz, nc, L, SSM_GROUPS, HEADS_PER_GROUP, SSM_HEAD_DIM)
    ad = jnp.moveaxis((dt * a).reshape(bsz, nc, L, SSM_GROUPS, HEADS_PER_GROUP), 2, -1)
    acum = jnp.cumsum(ad, axis=-1)
    bc = bm.reshape(bsz, nc, L, SSM_GROUPS, D_STATE)
    cc = cm.reshape(bsz, nc, L, SSM_GROUPS, D_STATE)
    causal = jnp.tril(jnp.ones((L, L), dtype=bool))
    seg = acum[..., :, None] - acum[..., None, :]
    lmat = jnp.where(causal, jnp.exp(jnp.where(causal, seg, 0.0)), 0.0)
    cb = jnp.einsum('bclgn,bcsgn->bcgls', cc, bc)
    y_diag = jnp.einsum('bcgrls,bcsgrp->bclgrp', cb[:, :, :, None] * lmat, xg)
    decay = jnp.exp(acum[..., -1:] - acum)
    states = jnp.einsum('bclgn,bcgrl,bclgrp->bcgrpn', bc, decay, xg)
    chunk_decay = jnp.exp(acum[..., -1])

    def step(h, inp):
        st, d = inp
        return h * d[..., None, None] + st, h

    h_init = h0.reshape(bsz, SSM_GROUPS, HEADS_PER_GROUP, SSM_HEAD_DIM, D_STATE)
    h_last, h_prev = lax.scan(step, h_init, (jnp.moveaxis(states, 1, 0), jnp.moveaxis(chunk_decay, 1, 0)))
    h_prev = jnp.moveaxis(h_prev, 0, 1)
    y_off = jnp.einsum('bclgn,bcgrpn,bcgrl->bclgrp', cc, h_prev, jnp.exp(acum))
    y = (y_diag + y_off).reshape(bsz, nc * L, SSM_HEADS, SSM_HEAD_DIM)[:, :t]
    return y, h_last.reshape(bsz, SSM_HEADS, SSM_HEAD_DIM, D_STATE)


def mamba_branch(z, xbc, dt_raw, conv_buf, h0, conv_w, conv_b, dt_bias, a_log, d_skip, ssm_norm):
    f32 = jnp.float32
    b, t = xbc.shape[:2]
    full = jnp.concatenate([conv_buf.astype(xbc.dtype), xbc], axis=1)
    conv = lax.conv_general_dilated(full, conv_w[:, None, :].astype(xbc.dtype), window_strides=(1,), padding='VALID',
                                    dimension_numbers=('NWC', 'WIO', 'NWC'), feature_group_count=CONV_DIM)
    u = jax.nn.silu(conv + conv_b)
    new_conv = full[:, -(CONV_W - 1):]
    xs, bm, cm = jnp.split(u, [D_INNER, D_INNER + SSM_GROUPS * D_STATE], axis=-1)
    xs = xs.reshape(b, t, SSM_HEADS, SSM_HEAD_DIM).astype(f32)
    bm = bm.reshape(b, t, SSM_GROUPS, D_STATE).astype(f32)
    cm = cm.reshape(b, t, SSM_GROUPS, D_STATE).astype(f32)
    dt = jax.nn.softplus(dt_raw.astype(f32) + dt_bias.astype(f32))
    a = -jnp.exp(a_log.astype(f32))
    y, h_last = ssd_scan(xs, dt, a, bm, cm, h0.astype(f32))
    y = y + d_skip.astype(f32)[:, None] * xs
    y = y.reshape(b, t, D_INNER) * jax.nn.silu(z.astype(f32))
    y = rmsnorm(y.reshape(b, t, SSM_GROUPS, D_INNER // SSM_GROUPS), ssm_norm.reshape(SSM_GROUPS, -1)).reshape(b, t, D_INNER)
    return y.astype(z.dtype), new_conv, h_last


def hier_moe(x, w_rg, b_rg, w_re, b_re, w_eg, w_eu, w_ed):
    f32 = jnp.float32
    n = x.shape[0]
    rows = jnp.arange(n)
    g_logits = (x @ w_rg).astype(f32) + b_rg.astype(f32)
    g_idx = jnp.argmax(g_logits, axis=-1)
    g_w = jax.nn.softmax(g_logits, axis=-1)[rows, g_idx][:, None]
    e_logits = ((x @ w_re).astype(f32) + b_re.astype(f32)).reshape(n, N_EXPERT_GROUPS, EXPERTS_PER_GROUP)[rows, g_idx]
    top_v, top_i = lax.top_k(e_logits, TOP_K_IN_GROUP)
    top_w = jax.nn.softmax(top_v, axis=-1) * g_w
    eid = g_idx[:, None] * EXPERTS_PER_GROUP + top_i
    comb = jnp.einsum('nk,nke->ne', top_w, jax.nn.one_hot(eid, N_EXPERTS, dtype=f32)).astype(x.dtype)
    y = jnp.zeros_like(x)
    for gi in range(N_EXPERT_GROUPS):
        sl = slice(gi * EXPERTS_PER_GROUP, (gi + 1) * EXPERTS_PER_GROUP)
        hg = jax.nn.silu(jnp.einsum('nd,edf->nef', x, w_eg[sl])) * jnp.einsum('nd,edf->nef', x, w_eu[sl])
        y = y + jnp.einsum('nef,efd->nd', hg * comb[:, sl, None], w_ed[sl])
    return y


def finish(x, attn, ssm, merge_logits, w_branch_attn, w_branch_ssm, w_out, norm2, w_rg, b_rg, w_re, b_re, w_eg, w_eu, w_ed):
    ga, gm = jnp.split(jax.nn.sigmoid(merge_logits.astype(jnp.float32)).astype(x.dtype), 2, axis=-1)
    mixed = ga * (attn @ w_branch_attn) + gm * (ssm @ w_branch_ssm)
    h = x + mixed @ w_out
    b, t, d = h.shape
    hn = rmsnorm(h, norm2).reshape(b * t, d)
    return h + hier_moe(hn, w_rg, b_rg, w_re, b_re, w_eg, w_eu, w_ed).reshape(b, t, d)


def setup_inputs(seed: int = 0) -> dict:
    key = jax.random.key(seed)
    ks = jax.random.split(key, 40)
    f32 = jnp.float32
    n_pages = PAST_LEN // PAGE_SIZE
    n_used = DEC_BATCH * n_pages
    n_pool = n_used + (n_used + 3) // 4
    page_table = jax.random.permutation(ks[0], n_pool)[:n_used].reshape(DEC_BATCH, n_pages).astype(jnp.int32)
    win_len = min(WINDOW, PAST_LEN)

    def normal(k, shape, scale):
        return jax.random.normal(k, shape, f32) * scale

    def gain(k, shape):
        return 1.0 + 0.02 * jax.random.normal(k, shape, f32)

    dt0 = jnp.exp(jax.random.uniform(ks[16], (DEPTH, SSM_HEADS), f32, math.log(1e-3), math.log(1e-1)))
    return {
        'x_prompt': normal(ks[1], (BATCH, SEQ, D_MODEL), 1.0),
        'x_sample': normal(ks[2], (DEC_BATCH, DEC_SEQ, D_MODEL), 1.0),
        'cache_kv': normal(ks[3], (DEPTH, n_pool, PAGE_SIZE, KV_SLOTS, KV_HEADS, HEAD_DIM), 1.0),
        'cache_win_kv': normal(ks[4], (DEPTH, DEC_BATCH, win_len, 2, KV_HEADS, HEAD_DIM), 1.0),
        'state_conv': normal(ks[5], (DEPTH, DEC_BATCH, CONV_W - 1, CONV_DIM), 1.0),
        'state_ssm': normal(ks[6], (DEPTH, DEC_BATCH, SSM_HEADS, SSM_HEAD_DIM, D_STATE), 0.1),
        'page_table': page_table,
        'norm1': gain(ks[7], (DEPTH, D_MODEL)),
        'w_in': normal(ks[8], (DEPTH, D_MODEL, D_IN_PROJ), D_MODEL ** -0.5),
        'q_norm': gain(ks[9], (DEPTH, HEAD_DIM)),
        'k_norm': gain(ks[10], (DEPTH, 3, HEAD_DIM)),
        'cmp_pe': normal(ks[11], (DEPTH, 2, CMP_BLOCK, HEAD_DIM), 0.1),
        'cmp_w1': normal(ks[12], (DEPTH, 2, CMP_BLOCK * HEAD_DIM, CMP_HIDDEN), (CMP_BLOCK * HEAD_DIM) ** -0.5),
        'cmp_w2': normal(ks[13], (DEPTH, 2, CMP_HIDDEN, HEAD_DIM), CMP_HIDDEN ** -0.5),
        'conv_w': normal(ks[14], (DEPTH, CONV_W, CONV_DIM), CONV_W ** -0.5),
        'conv_b': normal(ks[15], (DEPTH, CONV_DIM), 0.02),
        'dt_bias': dt0 + jnp.log(-jnp.expm1(-dt0)),
        'a_log': jnp.log(jax.random.uniform(ks[17], (DEPTH, SSM_HEADS), f32, 1.0, 16.0)),
        'd_skip': gain(ks[18], (DEPTH, SSM_HEADS)),
        'ssm_norm': gain(ks[19], (DEPTH, D_INNER)),
        'w_branch_attn': normal(ks[20], (DEPTH, N_HEADS * HEAD_DIM, D_MODEL), (N_HEADS * HEAD_DIM) ** -0.5),
        'w_branch_ssm': normal(ks[21], (DEPTH, D_INNER, D_MODEL), D_INNER ** -0.5),
        'w_out': normal(ks[22], (DEPTH, D_MODEL, D_MODEL), D_MODEL ** -0.5),
        'norm2': gain(ks[23], (DEPTH, D_MODEL)),
        'w_router_group': normal(ks[24], (DEPTH, D_MODEL, N_EXPERT_GROUPS), D_MODEL ** -0.5),
        'b_router_group': normal(ks[25], (DEPTH, N_EXPERT_GROUPS), 0.01),
        'w_router_expert': normal(ks[26], (DEPTH, D_MODEL, N_EXPERTS), D_MODEL ** -0.5),
        'b_router_expert': normal(ks[27], (DEPTH, N_EXPERTS), 0.01),
        'w_exp_gate': normal(ks[28], (DEPTH, N_EXPERTS, D_MODEL, D_EXPERT), D_MODEL ** -0.5),
        'w_exp_up': normal(ks[29], (DEPTH, N_EXPERTS, D_MODEL, D_EXPERT), D_MODEL ** -0.5),
        'w_exp_down': normal(ks[30], (DEPTH, N_EXPERTS, D_EXPERT, D_MODEL), D_EXPERT ** -0.5),
    }


def reference(x_prompt, x_sample, cache_kv, cache_win_kv, state_conv, state_ssm, page_table, norm1, w_in, q_norm, k_norm,
              cmp_pe, cmp_w1, cmp_w2, conv_w, conv_b, dt_bias, a_log, d_skip, ssm_norm, w_branch_attn, w_branch_ssm, w_out,
              norm2, w_router_group, b_router_group, w_router_expert, b_router_expert, w_exp_gate, w_exp_up, w_exp_down):
    slopes = alibi_slopes()
    bp, sp = x_prompt.shape[:2]
    bs = x_sample.shape[0]
    hp, hs = x_prompt, x_sample
    kv_p, win_p, conv_p, ssm_p = [], [], [], []
    kv_s, win_s, conv_s, ssm_s = [], [], [], []
    for l in range(DEPTH):
        q, gate, rows, win, z, xbc, dtr, mg = mixer_inputs(hp, norm1[l], w_in[l], q_norm[l], k_norm[l])
        attn = nsa_prompt(q, gate, rows, win, slopes, k_norm[l], cmp_pe[l], cmp_w1[l], cmp_w2[l])
        ssm, cbuf, hstate = mamba_branch(z, xbc, dtr, jnp.zeros((bp, CONV_W - 1, CONV_DIM), hp.dtype),
                                         jnp.zeros((bp, SSM_HEADS, SSM_HEAD_DIM, D_STATE), jnp.float32),
                                         conv_w[l], conv_b[l], dt_bias[l], a_log[l], d_skip[l], ssm_norm[l])
        kv_p.append(jnp.stack(rows, axis=2))
        win_p.append(jnp.stack(win, axis=2)[:, -min(WINDOW, sp):])
        conv_p.append(cbuf)
        ssm_p.append(hstate)
        hp = finish(hp, attn, ssm, mg, w_branch_attn[l], w_branch_ssm[l], w_out[l], norm2[l], w_router_group[l],
                    b_router_group[l], w_router_expert[l], b_router_expert[l], w_exp_gate[l], w_exp_up[l], w_exp_down[l])
        q, gate, rows, win, z, xbc, dtr, mg = mixer_inputs(hs, norm1[l], w_in[l], q_norm[l], k_norm[l])
        past = [cache_kv[l, page_table, :, si].reshape(bs, -1, KV_HEADS, HEAD_DIM).astype(hs.dtype) for si in range(KV_SLOTS)]
        attn, new_win = nsa_sample(q, gate, rows, win, past, cache_win_kv[l], slopes, k_norm[l], cmp_pe[l], cmp_w1[l], cmp_w2[l])
        ssm, cbuf, hstate = mamba_branch(z, xbc, dtr, state_conv[l], state_ssm[l], conv_w[l], conv_b[l], dt_bias[l],
                                         a_log[l], d_skip[l], ssm_norm[l])
        kv_s.append(jnp.stack(rows, axis=2))
        win_s.append(new_win)
        conv_s.append(cbuf)
        ssm_s.append(hstate)
        hs = finish(hs, attn, ssm, mg, w_branch_attn[l], w_branch_ssm[l], w_out[l], norm2[l], w_router_group[l],
                    b_router_group[l], w_router_expert[l], b_router_expert[l], w_exp_gate[l], w_exp_up[l], w_exp_down[l])
    return (hp, hs, jnp.stack(kv_p), jnp.stack(win_p), jnp.stack(conv_p), jnp.stack(ssm_p),
            jnp.stack(kv_s), jnp.stack(win_s), jnp.stack(conv_s), jnp.stack(ssm_s))
```

```python
import functools

import jax
import jax.numpy as jnp
import numpy as np
from jax import lax
from jax.experimental import pallas as pl
from jax.experimental.pallas import tpu as pltpu

F32 = jnp.float32
BF16 = jnp.bfloat16

D_MODEL = 1024
N_HEADS = 16
HEAD_DIM = 64
KV_HEADS = 2
GROUP = N_HEADS // KV_HEADS
CMP_BLOCK = 32
CMP_STRIDE = 16
CMP_HIDDEN = 2 * HEAD_DIM
SEL_BLOCK = 64
N_SELECT = 16
N_LOCAL = 2
WINDOW = 512
PAGE_SIZE = 128
D_INNER = D_MODEL
SSM_HEAD_DIM = 64
SSM_HEADS = D_INNER // SSM_HEAD_DIM
SSM_GROUPS = 4
D_STATE = 128
CONV_W = 4
CONV_DIM = D_INNER + 2 * SSM_GROUPS * D_STATE
SSD_CHUNK = 128
N_EXPERT_GROUPS = 4
EXPERTS_PER_GROUP = 8
N_EXPERTS = N_EXPERT_GROUPS * EXPERTS_PER_GROUP
D_EXPERT = 512
EPS = 1e-6

LANES = 128
HALF = LANES // 2
NEG = -1e30
Y_COLS = 7 * 1024
MISC_GATE = 3 * N_HEADS
VMEM_LIMIT = 56 * 1024 * 1024


def _nn(a, b):
    return jnp.dot(a, b, preferred_element_type=F32)


def _nt(a, b):
    return lax.dot_general(a, b, (((1,), (1,)), ((), ())), preferred_element_type=F32)


def _sigmoid(x):
    return 1.0 / (1.0 + jnp.exp(-x))


def _lane_lo(shape):
    return lax.broadcasted_iota(jnp.int32, shape, 1) < HALF


def _swap_halves(x):
    return pltpu.roll(x, HALF, axis=1)


def _headnorm128(blk, gain):
    lo = _lane_lo(blk.shape)
    sq = blk * blk
    s_lo = jnp.sum(jnp.where(lo, sq, 0.0), axis=-1, keepdims=True)
    s_hi = jnp.sum(jnp.where(lo, 0.0, sq), axis=-1, keepdims=True)
    inv = jnp.where(lo, lax.rsqrt(s_lo * (1.0 / HALF) + EPS), lax.rsqrt(s_hi * (1.0 / HALF) + EPS))
    return blk * inv * gain


def _split_bf16(x, parts):
    out = []
    r = x
    for _ in range(parts):
        p = r.astype(BF16)
        out.append(p)
        r = r - p.astype(F32)
    return out


def _inproj_kernel(x_ref, n1_ref, w_ref, qg_ref, kg_ref, y_ref, xn_ref):
    j = pl.program_id(1)

    @pl.when(j == 0)
    def _():
        x = x_ref[...]
        inv = lax.rsqrt(jnp.mean(x * x, axis=-1, keepdims=True) + EPS)
        xn_ref[...] = (x * inv * n1_ref[...]).astype(BF16)

    acc = _nn(xn_ref[...], w_ref[...])

    @pl.when(j == 0)
    def _():
        for c in range(D_MODEL // LANES):
            sl = slice(c * LANES, (c + 1) * LANES)
            y_ref[:, sl] = _headnorm128(acc[:, sl], qg_ref[...]) * (HEAD_DIM ** -0.5)

    @pl.when(j == 1)
    def _():
        y_ref[...] = acc
        y_ref[:, 256:384] = _headnorm128(acc[:, 256:384], kg_ref[0:1, :])
        y_ref[:, 512:640] = _headnorm128(acc[:, 512:640], kg_ref[1:2, :])
        misc = acc[:, 768:896]
        lane = lax.broadcasted_iota(jnp.int32, misc.shape, 1)
        y_ref[:, 768:896] = jnp.where(lane < MISC_GATE, _sigmoid(misc), misc)

    @pl.when((j >= 2) & (j <= 4))
    def _():
        y_ref[...] = acc

    @pl.when(j >= 5)
    def _():
        y_ref[...] = _sigmoid(acc)


def _inproj(x2d, n1, w_perm, qg, kg, tm):
    n = x2d.shape[0]
    return pl.pallas_call(
        _inproj_kernel,
        grid=(n // tm, Y_COLS // 1024),
        in_specs=[
            pl.BlockSpec((tm, D_MODEL), lambda i, j: (i, 0)),
            pl.BlockSpec((1, D_MODEL), lambda i, j: (0, 0)),
            pl.BlockSpec((D_MODEL, 1024), lambda i, j: (0, j)),
            pl.BlockSpec((1, LANES), lambda i, j: (0, 0)),
            pl.BlockSpec((2, LANES), lambda i, j: (0, 0)),
        ],
        out_specs=pl.BlockSpec((tm, 1024), lambda i, j: (i, j)),
        out_shape=jax.ShapeDtypeStruct((n, Y_COLS), F32),
        scratch_shapes=[pltpu.VMEM((tm, D_MODEL), BF16)],
        compiler_params=pltpu.CompilerParams(
            dimension_semantics=("arbitrary", "arbitrary"), vmem_limit_bytes=VMEM_LIMIT),
        name="inproj",
    )(x2d, n1, w_perm, qg, kg)


def _page_copy(pages_ref, buf_ref, sem, page, slot, p, blk, lane0):
    width = buf_ref.shape[3]
    return pltpu.make_async_copy(
        pages_ref.at[page, :, pl.ds(lane0 + blk * width, width)],
        buf_ref.at[slot, blk, pl.ds(pl.multiple_of(p * PAGE_SIZE, PAGE_SIZE), PAGE_SIZE), :],
        sem.at[slot])


def _start_page_copies(pages_ref, buf_ref, sem, pt_ref, seq, slot, npg, lane0):
    def body(p, c):
        page = pt_ref[seq, p]
        for blk in range(buf_ref.shape[1]):
            _page_copy(pages_ref, buf_ref, sem, page, slot, p, blk, lane0).start()
        return c
    lax.fori_loop(0, npg, body, 0)


def _wait_page_copies(pages_ref, buf_ref, sem, slot, npg, lane0):
    def body(p, c):
        for blk in range(buf_ref.shape[1]):
            _page_copy(pages_ref, buf_ref, sem, 0, slot, p, blk, lane0).wait()
        return c
    lax.fori_loop(0, npg, body, 0)


def _gather_step(pages_ref, buf_ref, sem, pt_ref, npg, lane0):
    i = pl.program_id(0)
    n = pl.num_programs(0)
    slot = i % 2

    @pl.when(i == 0)
    def _():
        _start_page_copies(pages_ref, buf_ref, sem, pt_ref, 0, 0, npg, lane0)

    @pl.when(i + 1 < n)
    def _():
        _start_page_copies(pages_ref, buf_ref, sem, pt_ref, i + 1, 1 - slot, npg, lane0)

    _wait_page_copies(pages_ref, buf_ref, sem, slot, npg, lane0)
    return slot


def _gelu_tanh(x):
    return 0.5 * x * (1.0 + jnp.tanh(0.7978845608028654 * (x + 0.044715 * x * x * x)))


def _compress_kernel(pt_ref, pages_ref, wcat_ref, w1_ref, pe_ref, w2lo_ref, w2hi_ref, kn_ref,
                     kc_ref, vc_ref, buf_ref, lt_ref, sem, *, npg, lane0):
    slot = _gather_step(pages_ref, buf_ref, sem, pt_ref, npg, lane0)
    nch = npg * (PAGE_SIZE // CMP_STRIDE)
    lo = _lane_lo((nch, LANES))
    lt_ref[pl.ds(nch, 8), :] = jnp.zeros((8, 4 * LANES), F32)
    for s in range(2):
        ze, zo = [], []
        for jp in range(CMP_STRIDE // 2):
            xa = buf_ref[slot, s, pl.ds(2 * jp, nch, stride=CMP_STRIDE), :]
            xb = buf_ref[slot, s, pl.ds(2 * jp + 1, nch, stride=CMP_STRIDE), :]
            ze.append(jnp.where(lo, xa, _swap_halves(xb)).astype(BF16))
            zo.append(jnp.where(lo, _swap_halves(xa), xb).astype(BF16))
        z = jnp.concatenate([jnp.concatenate(ze, axis=1), jnp.concatenate(zo, axis=1)], axis=0)
        lt = _nn(z, wcat_ref[s])
        lt_ref[pl.ds(0, nch), pl.ds(0, 2 * LANES)] = lt[:nch]
        lt_ref[pl.ds(0, nch), pl.ds(2 * LANES, 2 * LANES)] = lt[nch:]
        pb = _nn(pe_ref[s].astype(BF16), w1_ref[s])[0:1, :]
        hid = []
        for g in range(KV_HEADS):
            lead = lt_ref[pl.ds(0, nch), pl.ds(g * 2 * LANES, LANES)]
            trail = lt_ref[pl.ds(1, nch), pl.ds(g * 2 * LANES + LANES, LANES)]
            hid.append(_gelu_tanh(lead + trail + pb).astype(BF16))
        out = _nn(hid[0], w2lo_ref[s]) + _nn(hid[1], w2hi_ref[s])
        if s == 0:
            kc_ref[0] = _headnorm128(out, kn_ref[...])
        else:
            vc_ref[0] = out


def _compress(pages, page_table, lane0, wcat, w1b, pe8, w2lo, w2hi, kn0):
    nseq, npg = page_table.shape
    nch = npg * (PAGE_SIZE // CMP_STRIDE)
    kern = functools.partial(_compress_kernel, npg=npg, lane0=lane0)
    const3 = lambda i, pt: (0, 0, 0)
    return pl.pallas_call(
        kern,
        grid_spec=pltpu.PrefetchScalarGridSpec(
            num_scalar_prefetch=1,
            grid=(nseq,),
            in_specs=[
                pl.BlockSpec(memory_space=pl.ANY),
                pl.BlockSpec(wcat.shape, const3),
                pl.BlockSpec(w1b.shape, const3),
                pl.BlockSpec(pe8.shape, const3),
                pl.BlockSpec(w2lo.shape, const3),
                pl.BlockSpec(w2hi.shape, const3),
                pl.BlockSpec((1, LANES), lambda i, pt: (0, 0)),
            ],
            out_specs=[
                pl.BlockSpec((1, nch, LANES), lambda i, pt: (i, 0, 0)),
                pl.BlockSpec((1, nch, LANES), lambda i, pt: (i, 0, 0)),
            ],
            scratch_shapes=[
                pltpu.VMEM((2, 2, npg * PAGE_SIZE, LANES), F32),
                pltpu.VMEM((nch + 8, 4 * LANES), F32),
                pltpu.SemaphoreType.DMA((2,)),
            ],
        ),
        out_shape=[jax.ShapeDtypeStruct((nseq, nch, LANES), F32)] * 2,
        compiler_params=pltpu.CompilerParams(
            dimension_semantics=("arbitrary",), vmem_limit_bytes=VMEM_LIMIT),
        name="compress",
    )(page_table, pages, wcat, w1b, pe8, w2lo, w2hi, kn0)


def _head_rows(q, tq):
    lo = _lane_lo((tq, LANES))
    blocks = []
    for h in range(N_HEADS):
        g, j, odd = h // GROUP, h // 2, h % 2
        qp = q[:, j * LANES:(j + 1) * LANES]
        src = qp if odd == g else _swap_halves(qp)
        blocks.append(jnp.where(lo if g == 0 else ~lo, src, 0.0))
    return blocks


def _col16(pieces):
    return jnp.concatenate(pieces, axis=0)


def _cmp_and_select(qbd, kc, vc, cover, slope_col, qpos_col, qpos2, base, tq, nc_valid, nb_lanes, n_top, new_block):
    nc = kc.shape[0]
    n_row = lax.broadcasted_iota(jnp.int32, (1, nc), 1)
    cend = n_row * CMP_STRIDE + (CMP_BLOCK - 1)
    s = _nt(qbd, kc) + slope_col * (cend - base).astype(F32)
    valid = (cend <= qpos_col) & (n_row < nc_valid)
    s = jnp.where(valid, s, NEG)
    m = jnp.max(s, axis=-1, keepdims=True)
    p = jnp.where(valid, jnp.exp(s - m), 0.0)
    pc = p / jnp.maximum(jnp.sum(p, axis=-1, keepdims=True), 1e-30)
    o_c = _nn(pc.astype(BF16), vc)
    ps = []
    for g in range(KV_HEADS):
        acc = pc[(g * GROUP) * tq:(g * GROUP + 1) * tq]
        for r in range(1, GROUP):
            acc = acc + pc[(g * GROUP + r) * tq:(g * GROUP + r + 1) * tq]
        ps.append(acc)
    ps = jnp.concatenate(ps, axis=0)
    hi, lo = _split_bf16(ps, 2)
    imp = _nn(hi, cover) + _nn(lo, cover)
    jb = lax.broadcasted_iota(jnp.int32, imp.shape, 1)
    back = qpos2 // SEL_BLOCK - jb
    forced = (jb == 0) | ((back >= 0) & (back < N_LOCAL))
    impa = jnp.where(forced, jnp.inf, jnp.where(back < 0, -jnp.inf, imp))
    rank = jnp.zeros(imp.shape, jnp.int32)
    for k in range(nb_lanes):
        ck = impa[:, k:k + 1]
        beats = (ck > impa) | ((ck == impa) & (jb > k))
        rank = rank + beats.astype(jnp.int32)
    if new_block:
        rank = rank + jnp.where(impa == jnp.inf, 0, 1)
    selneg = jnp.where(rank < n_top, 0.0, NEG)
    return o_c, selneg


def _flash_init(m_s, l_s, acc_s):
    m_s[...] = jnp.full(m_s.shape, NEG, F32)
    l_s[...] = jnp.zeros(l_s.shape, F32)
    acc_s[...] = jnp.zeros(acc_s.shape, F32)


def _flash_update(qa, ka, v, bias, mask, m_s, l_s, acc_s):
    s = _nt(qa, ka) + bias
    if mask is not None:
        s = jnp.where(mask, s, NEG)
    m_old = m_s[...]
    m_new = jnp.maximum(m_old, jnp.max(s, axis=-1, keepdims=True))
    alpha = jnp.exp(m_old - m_new)
    p = jnp.where(s > 0.5 * NEG, jnp.exp(s - m_new), 0.0)
    l_s[...] = alpha * l_s[...] + jnp.sum(p, axis=-1, keepdims=True)
    acc_s[...] = alpha * acc_s[...] + _nn(p.astype(BF16), v)
    m_s[...] = m_new


def _flash_result(l_s, acc_s):
    return acc_s[...] / jnp.maximum(l_s[...], 1e-30)


def _merge_heads(o_c, o_s, o_w, gates, tq):
    lo = _lane_lo((tq, LANES))
    per_head = []
    for h in range(N_HEADS):
        rows = slice(h * tq, (h + 1) * tq)
        per_head.append(gates[:, 3 * h:3 * h + 1] * o_c[rows] + gates[:, 3 * h + 1:3 * h + 2] * o_s[rows]
                        + gates[:, 3 * h + 2:3 * h + 3] * o_w[rows])
    out = []
    for j in range(N_HEADS // 2):
        ev, od = per_head[2 * j], per_head[2 * j + 1]
        if j // (GROUP // 2) == 0:
            out.append(jnp.where(lo, ev, _swap_halves(od)))
        else:
            out.append(jnp.where(lo, _swap_halves(ev), od))
    return out


def _slope_col(slopes_ref, tq):
    return _col16([jnp.broadcast_to(slopes_ref[h:h + 1, 0:1], (tq, 1)) for h in range(N_HEADS)])


def _nsa_prompt_kernel(q_ref, kv_ref, win_ref, misc_ref, kc_ref, vc_ref, et_ref, cover_ref, slopes_ref,
                       o_ref, m_s, l_s, acc_s, *, tq, tk, seq):
    start = pl.program_id(1) * tq
    qbd = _col16(_head_rows(q_ref[0], tq)).astype(BF16)
    slope_col = _slope_col(slopes_ref, tq)
    tpos = start + lax.broadcasted_iota(jnp.int32, (tq, 1), 0)
    qpos_col = _col16([tpos] * N_HEADS)
    qpos2 = _col16([tpos] * KV_HEADS)
    nb = -(-seq // SEL_BLOCK)
    o_c, selneg = _cmp_and_select(
        qbd, kc_ref[0].astype(BF16), vc_ref[0].astype(BF16), cover_ref[...], slope_col, qpos_col, qpos2, start, tq,
        seq // CMP_STRIDE - 1, nb, min(N_SELECT, nb), False)
    selrep = _col16([selneg[(h // GROUP) * tq:(h // GROUP + 1) * tq] for h in range(N_HEADS)]).astype(BF16)
    qaug = jnp.concatenate([qbd, selrep], axis=1)

    def tile(kt):
        off = pl.multiple_of(kt * tk, tk)
        kpos = off + lax.broadcasted_iota(jnp.int32, (1, tk), 1)
        return off, kpos, slope_col * (kpos - start).astype(F32)

    _flash_init(m_s, l_s, acc_s)

    def sel_body(kt, c):
        off, kpos, bias = tile(kt)
        ka = jnp.concatenate([kv_ref[0, pl.ds(off, tk), 256:384].astype(BF16), et_ref[pl.ds(off, tk), :]], axis=1)
        v = kv_ref[0, pl.ds(off, tk), 384:512].astype(BF16)
        _flash_update(qaug, ka, v, bias, qpos_col >= kpos, m_s, l_s, acc_s)
        return c

    kt_end = (start + tq + tk - 1) // tk
    lax.fori_loop(0, kt_end, sel_body, 0)
    o_s = _flash_result(l_s, acc_s)

    _flash_init(m_s, l_s, acc_s)

    def win_body(kt, c):
        off, kpos, bias = tile(kt)
        k = win_ref[0, pl.ds(off, tk), 0:128].astype(BF16)
        v = win_ref[0, pl.ds(off, tk), 128:256].astype(BF16)
        d = qpos_col - kpos
        _flash_update(qbd, k, v, bias, (d >= 0) & (d < WINDOW), m_s, l_s, acc_s)
        return c

    lax.fori_loop(jnp.maximum(start - WINDOW, 0) // tk, kt_end, win_body, 0)
    o_w = _flash_result(l_s, acc_s)

    out = _merge_heads(o_c, o_s, o_w, misc_ref[0], tq)
    for j in range(N_HEADS // 2):
        o_ref[0, :, j * LANES:(j + 1) * LANES] = out[j]


def _nsa_prompt(y3, kc, vc, et, cover, slopes, tq, tk):
    b, seq, _ = y3.shape
    nc = kc.shape[1]
    m = N_HEADS * tq
    kern = functools.partial(_nsa_prompt_kernel, tq=tq, tk=tk, seq=seq)
    return pl.pallas_call(
        kern,
        grid=(b, seq // tq),
        in_specs=[
            pl.BlockSpec((1, tq, 1024), lambda i, j: (i, j, 0)),
            pl.BlockSpec((1, seq, 512), lambda i, j: (i, 0, 2)),
            pl.BlockSpec((1, seq, 256), lambda i, j: (i, 0, 6)),
            pl.BlockSpec((1, tq, LANES), lambda i, j: (i, j, 14)),
            pl.BlockSpec((1, nc, LANES), lambda i, j: (i, 0, 0)),
            pl.BlockSpec((1, nc, LANES), lambda i, j: (i, 0, 0)),
            pl.BlockSpec(et.shape, lambda i, j: (0, 0)),
            pl.BlockSpec(cover.shape, lambda i, j: (0, 0)),
            pl.BlockSpec(slopes.shape, lambda i, j: (0, 0)),
        ],
        out_specs=pl.BlockSpec((1, tq, 1024), lambda i, j: (i, j, 0)),
        out_shape=jax.ShapeDtypeStruct((b, seq, 1024), F32),
        scratch_shapes=[pltpu.VMEM((m, 1), F32), pltpu.VMEM((m, 1), F32), pltpu.VMEM((m, LANES), F32)],
        compiler_params=pltpu.CompilerParams(
            dimension_semantics=("arbitrary", "arbitrary"), vmem_limit_bytes=VMEM_LIMIT),
        name="nsa_prompt",
    )(y3, y3, y3, y3, kc, vc, et, cover, slopes)


def _pad_rows(x, rows):
    return jnp.concatenate([x, jnp.zeros((rows - x.shape[0], x.shape[1]), x.dtype)], axis=0)


def _nsa_sample_kernel(pt_ref, q_ref, new_ref, kc_ref, vc_ref, wbuf_ref, pages_ref, et_ref, cover_ref, slopes_ref,
                       o_ref, nwin_ref, buf_ref, m_s, l_s, acc_s, sem, *, ts, tk, npg, past):
    slot = _gather_step(pages_ref, buf_ref, sem, pt_ref, npg, 2 * LANES)
    qbd = _col16(_head_rows(q_ref[0], ts)).astype(BF16)
    slope_col = _slope_col(slopes_ref, ts)
    tpos = past + lax.broadcasted_iota(jnp.int32, (ts, 1), 0)
    qpos_col = _col16([tpos] * N_HEADS)
    qpos2 = _col16([tpos] * KV_HEADS)
    nb_past = past // SEL_BLOCK
    o_c, selneg = _cmp_and_select(
        qbd, kc_ref[0].astype(BF16), vc_ref[0].astype(BF16), cover_ref[...], slope_col, qpos_col, qpos2, past, ts,
        past // CMP_STRIDE - 1, nb_past, N_SELECT, True)
    selrep = _col16([selneg[(h // GROUP) * ts:(h // GROUP + 1) * ts] for h in range(N_HEADS)]).astype(BF16)
    qaug = jnp.concatenate([qbd, selrep], axis=1)

    new = new_ref[0]
    npos = past + lax.broadcasted_iota(jnp.int32, (1, LANES), 1)
    nbias = slope_col * (npos - past).astype(F32)
    nmask = (qpos_col >= npos) & (npos < past + ts)

    _flash_init(m_s, l_s, acc_s)

    def sel_body(kt, c):
        off = pl.multiple_of(kt * tk, tk)
        kpos = off + lax.broadcasted_iota(jnp.int32, (1, tk), 1)
        ka = jnp.concatenate([buf_ref[slot, 0, pl.ds(off, tk), 0:128].astype(BF16), et_ref[pl.ds(off, tk), :]], axis=1)
        v = buf_ref[slot, 0, pl.ds(off, tk), 128:256].astype(BF16)
        _flash_update(qaug, ka, v, slope_col * (kpos - past).astype(F32), None, m_s, l_s, acc_s)
        return c

    lax.fori_loop(0, past // tk, sel_body, 0)
    _flash_update(qbd, _pad_rows(new[:, 256:384], LANES).astype(BF16), _pad_rows(new[:, 384:512], LANES).astype(BF16),
                  nbias, nmask, m_s, l_s, acc_s)
    o_s = _flash_result(l_s, acc_s)

    _flash_init(m_s, l_s, acc_s)
    w0 = wbuf_ref.shape[1]
    wpos = (past - w0) + lax.broadcasted_iota(jnp.int32, (1, w0), 1)
    dw = qpos_col - wpos
    _flash_update(qbd, wbuf_ref[0, :, 0:128].astype(BF16), wbuf_ref[0, :, 128:256].astype(BF16),
                  slope_col * (wpos - past).astype(F32), (dw >= 0) & (dw < WINDOW) & (wpos >= 0), m_s, l_s, acc_s)
    _flash_update(qbd, _pad_rows(new[:, 512:640], LANES).astype(BF16), _pad_rows(new[:, 640:768], LANES).astype(BF16),
                  nbias, nmask, m_s, l_s, acc_s)
    o_w = _flash_result(l_s, acc_s)

    out = _merge_heads(o_c, o_s, o_w, new[:, 768:896], ts)
    for j in range(N_HEADS // 2):
        o_ref[0, :, j * LANES:(j + 1) * LANES] = out[j]
    nwin_ref[0, pl.ds(0, w0 - ts), :] = wbuf_ref[0, pl.ds(ts, w0 - ts), :]
    nwin_ref[0, pl.ds(w0 - ts, ts), :] = new[:, 512:768]


def _nsa_sample(page_table, ys3, kc, vc, wbuf, pages, et, cover, slopes, tk):
    nseq, ts, _ = ys3.shape
    npg = page_table.shape[1]
    past = npg * PAGE_SIZE
    nc = kc.shape[1]
    w0 = wbuf.shape[1]
    m = N_HEADS * ts
    kern = functools.partial(_nsa_sample_kernel, ts=ts, tk=tk, npg=npg, past=past)
    return pl.pallas_call(
        kern,
        grid_spec=pltpu.PrefetchScalarGridSpec(
            num_scalar_prefetch=1,
            grid=(nseq,),
            in_specs=[
                pl.BlockSpec((1, ts, 1024), lambda i, pt: (i, 0, 0)),
                pl.BlockSpec((1, ts, 1024), lambda i, pt: (i, 0, 1)),
                pl.BlockSpec((1, nc, LANES), lambda i, pt: (i, 0, 0)),
                pl.BlockSpec((1, nc, LANES), lambda i, pt: (i, 0, 0)),
                pl.BlockSpec((1, w0, 256), lambda i, pt: (i, 0, 0)),
                pl.BlockSpec(memory_space=pl.ANY),
                pl.BlockSpec(et.shape, lambda i, pt: (0, 0)),
                pl.BlockSpec(cover.shape, lambda i, pt: (0, 0)),
                pl.BlockSpec(slopes.shape, lambda i, pt: (0, 0)),
            ],
            out_specs=[
                pl.BlockSpec((1, ts, 1024), lambda i, pt: (i, 0, 0)),
                pl.BlockSpec((1, w0, 256), lambda i, pt: (i, 0, 0)),
            ],
            scratch_shapes=[
                pltpu.VMEM((2, 1, past, 2 * LANES), F32),
                pltpu.VMEM((m, 1), F32), pltpu.VMEM((m, 1), F32), pltpu.VMEM((m, LANES), F32),
                pltpu.SemaphoreType.DMA((2,)),
            ],
        ),
        out_shape=[jax.ShapeDtypeStruct((nseq, ts, 1024), F32), jax.ShapeDtypeStruct((nseq, w0, 256), F32)],
        compiler_params=pltpu.CompilerParams(
            dimension_semantics=("arbitrary",), vmem_limit_bytes=VMEM_LIMIT),
        name="nsa_sample",
    )(page_table, ys3, ys3, kc, vc, wbuf, pages, et, cover, slopes)


def _softplus(x):
    return jnp.maximum(x, 0.0) + jnp.log(1.0 + jnp.exp(-jnp.abs(x)))


def _ssd_kernel(xbc_ref, z_ref, misc_ref, cs_ref, h0_ref, cw_ref, cb_ref, dtb_ref, alog_ref, dsk_ref, gn_ref, tri_ref,
                y_ref, cso_ref, ho_ref, xf_s, ht_s, *, L, TR):
    c = pl.program_id(1)
    nchunks = pl.num_programs(1)
    npair = SSM_HEADS // 2

    @pl.when(c == 0)
    def _():
        xf_s[0:8, :] = jnp.zeros((8, CONV_DIM), F32)
        xf_s[8 - (CONV_W - 1):8, :] = cs_ref[0]
        for j in range(npair):
            ht_s[j] = jnp.concatenate([h0_ref[0, 2 * j], h0_ref[0, 2 * j + 1]], axis=0).T

    @pl.when(c > 0)
    def _():
        xf_s[0:8, :] = xf_s[L:L + 8, :]

    x = xbc_ref[0]
    z = z_ref[0]
    misc = misc_ref[0]
    if TR < L:
        x, z, misc = _pad_rows(x, L), _pad_rows(z, L), _pad_rows(misc, L)
    xf_s[8:8 + L, :] = x
    conv = cb_ref[...]
    for k in range(CONV_W):
        conv = conv + cw_ref[k:k + 1, :] * xf_s[pl.ds(8 - (CONV_W - 1) + k, L), :]
    u = conv * _sigmoid(conv)
    xs = u[:, 0:D_INNER]
    bm = u[:, D_INNER:D_INNER + SSM_GROUPS * D_STATE]
    cm = u[:, D_INNER + SSM_GROUPS * D_STATE:]

    dt = _softplus(misc + dtb_ref[...])
    if TR < L:
        dt = jnp.where(lax.broadcasted_iota(jnp.int32, dt.shape, 0) < TR, dt, 0.0)
    ad = dt * (-jnp.exp(alog_ref[...]))
    tri = tri_ref[...]
    acum = sum(_nn(tri, part) for part in _split_bf16(ad, 3))
    acum_t = acum.T
    tot = acum[L - 1:L, :]
    rows = lax.broadcasted_iota(jnp.int32, (L, L), 0)
    cols = lax.broadcasted_iota(jnp.int32, (L, L), 1)
    causal = rows >= cols
    lo = _lane_lo((L, LANES))
    lo1 = _lane_lo((1, LANES))
    d0 = MISC_GATE

    ys = []
    for g in range(SSM_GROUPS):
        bc = bm[:, g * D_STATE:(g + 1) * D_STATE]
        ccb = cm[:, g * D_STATE:(g + 1) * D_STATE].astype(BF16)
        cb = _nt(ccb, bc.astype(BF16))
        bct = bc.T.astype(BF16)
        for jj in range(2):
            j = 2 * g + jj
            he, ho = 2 * j, 2 * j + 1
            a_e, a_o = acum[:, d0 + he:d0 + he + 1], acum[:, d0 + ho:d0 + ho + 1]
            r_e, r_o = acum_t[d0 + he:d0 + he + 1, :], acum_t[d0 + ho:d0 + ho + 1, :]
            t_e, t_o = tot[:, d0 + he:d0 + he + 1], tot[:, d0 + ho:d0 + ho + 1]
            lm_e = jnp.where(causal, jnp.exp(jnp.where(causal, a_e - r_e, 0.0)), 0.0)
            lm_o = jnp.where(causal, jnp.exp(jnp.where(causal, a_o - r_o, 0.0)), 0.0)
            xs_p = xs[:, j * LANES:(j + 1) * LANES]
            xg_p = xs_p * jnp.where(lo, dt[:, d0 + he:d0 + he + 1], dt[:, d0 + ho:d0 + ho + 1])
            y_p = (_nn((cb * lm_e).astype(BF16), jnp.where(lo, xg_p, 0.0).astype(BF16))
                   + _nn((cb * lm_o).astype(BF16), jnp.where(lo, 0.0, xg_p).astype(BF16)))
            h_prev = ht_s[j]
            y_p = y_p + jnp.where(lo, jnp.exp(a_e), jnp.exp(a_o)) * _nn(ccb, h_prev.astype(BF16))
            dec = jnp.where(lo, jnp.exp(t_e - a_e), jnp.exp(t_o - a_o))
            st = _nn(bct, (xg_p * dec).astype(BF16))
            ht_s[j] = h_prev * jnp.where(lo1, jnp.exp(t_e), jnp.exp(t_o)) + st
            ys.append(y_p + dsk_ref[:, j * LANES:(j + 1) * LANES] * xs_p)

    gw = D_INNER // SSM_GROUPS
    zg = z * _sigmoid(z)
    for gi in range(SSM_GROUPS):
        blk = jnp.concatenate(ys[gi * 2:gi * 2 + 2], axis=1) * zg[:, gi * gw:(gi + 1) * gw]
        inv = lax.rsqrt(jnp.mean(blk * blk, axis=-1, keepdims=True) + EPS)
        y_ref[0, :, gi * gw:(gi + 1) * gw] = (blk * inv * gn_ref[:, gi * gw:(gi + 1) * gw])[:TR]

    @pl.when(c == nchunks - 1)
    def _():
        cso_ref[0] = xf_s[pl.ds(8 + TR - (CONV_W - 1), CONV_W - 1), :]
        for j in range(npair):
            t = ht_s[j].T
            ho_ref[0, 2 * j] = t[0:SSM_HEAD_DIM]
            ho_ref[0, 2 * j + 1] = t[SSM_HEAD_DIM:]


def _ssd(y3, conv_state, h0, cw, cb, dtb, alog, dsk, gn, tri, L, TR):
    b, t, _ = y3.shape
    nchunks = t // TR
    kern = functools.partial(_ssd_kernel, L=L, TR=TR)
    c2 = lambda i, j: (0, 0)
    return pl.pallas_call(
        kern,
        grid=(b, nchunks),
        in_specs=[
            pl.BlockSpec((1, TR, CONV_DIM), lambda i, j: (i, j, 1)),
            pl.BlockSpec((1, TR, D_INNER), lambda i, j: (i, j, 4)),
            pl.BlockSpec((1, TR, LANES), lambda i, j: (i, j, 14)),
            pl.BlockSpec((1, CONV_W - 1, CONV_DIM), lambda i, j: (i, 0, 0)),
            pl.BlockSpec((1, SSM_HEADS, SSM_HEAD_DIM, D_STATE), lambda i, j: (i, 0, 0, 0)),
            pl.BlockSpec(cw.shape, c2), pl.BlockSpec(cb.shape, c2), pl.BlockSpec(dtb.shape, c2),
            pl.BlockSpec(alog.shape, c2), pl.BlockSpec(dsk.shape, c2), pl.BlockSpec(gn.shape, c2),
            pl.BlockSpec(tri.shape, c2),
        ],
        out_specs=[
            pl.BlockSpec((1, TR, D_INNER), lambda i, j: (i, j, 0)),
            pl.BlockSpec((1, CONV_W - 1, CONV_DIM), lambda i, j: (i, 0, 0)),
            pl.BlockSpec((1, SSM_HEADS, SSM_HEAD_DIM, D_STATE), lambda i, j: (i, 0, 0, 0)),
        ],
        out_shape=[
            jax.ShapeDtypeStruct((b, t, D_INNER), F32),
            jax.ShapeDtypeStruct((b, CONV_W - 1, CONV_DIM), F32),
            jax.ShapeDtypeStruct((b, SSM_HEADS, SSM_HEAD_DIM, D_STATE), F32),
        ],
        scratch_shapes=[pltpu.VMEM((L + 8, CONV_DIM), F32), pltpu.VMEM((SSM_HEADS // 2, D_STATE, LANES), F32)],
        compiler_params=pltpu.CompilerParams(
            dimension_semantics=("arbitrary", "arbitrary"), vmem_limit_bytes=VMEM_LIMIT),
        name="ssd",
    )(y3, y3, y3, conv_state, h0, cw, cb, dtb, alog, dsk, gn, tri)


def _finish_kernel(attn_ref, ssm_ref, ga_ref, gm_ref, x_ref, wa_ref, wm_ref, wo_ref, n2_ref, wrh_ref, wrl_ref, br_ref,
                   h_ref, hn_ref, comb_ref):
    a = _nn(attn_ref[...].astype(BF16), wa_ref[...])
    s = _nn(ssm_ref[...].astype(BF16), wm_ref[...])
    mixed = ga_ref[...] * a + gm_ref[...] * s
    h = x_ref[...] + _nn(mixed.astype(BF16), wo_ref[...])
    h_ref[...] = h
    hn = h * lax.rsqrt(jnp.mean(h * h, axis=-1, keepdims=True) + EPS) * n2_ref[...]
    hn_ref[...] = hn.astype(BF16)
    hh, hl = _split_bf16(hn, 2)
    logits = _nn(hh, wrh_ref[...]) + _nn(hh, wrl_ref[...]) + _nn(hl, wrh_ref[...]) + br_ref[...]

    lane = lax.broadcasted_iota(jnp.int32, logits.shape, 1)
    big = jnp.int32(4 * LANES)
    in_g = lane < N_EXPERT_GROUPS
    gl = jnp.where(in_g, logits, -jnp.inf)
    gmax = jnp.max(gl, axis=-1, keepdims=True)
    gidx = jnp.min(jnp.where(gl == gmax, lane, big), axis=-1, keepdims=True)
    gw = 1.0 / jnp.sum(jnp.where(in_g, jnp.exp(logits - gmax), 0.0), axis=-1, keepdims=True)
    e0 = N_EXPERT_GROUPS + EXPERTS_PER_GROUP * gidx
    el = jnp.where((lane >= e0) & (lane < e0 + EXPERTS_PER_GROUP), logits, -jnp.inf)
    v1 = jnp.max(el, axis=-1, keepdims=True)
    i1 = jnp.min(jnp.where(el == v1, lane, big), axis=-1, keepdims=True)
    el2 = jnp.where(lane == i1, -jnp.inf, el)
    v2 = jnp.max(el2, axis=-1, keepdims=True)
    i2 = jnp.min(jnp.where(el2 == v2, lane, big), axis=-1, keepdims=True)
    e = jnp.exp(v2 - v1)
    w1 = gw / (1.0 + e)
    comb_ref[...] = jnp.where(lane == i1, w1, jnp.where(lane == i2, w1 * e, 0.0))


def _finish(attn, ssm, y2d, x2d, wa, wm, wo, n2, wrh, wrl, br, tm):
    n = x2d.shape[0]
    c2 = lambda i: (0, 0)
    row = lambda i: (i, 0)
    return pl.pallas_call(
        _finish_kernel,
        grid=(n // tm,),
        in_specs=[
            pl.BlockSpec((tm, 1024), row), pl.BlockSpec((tm, 1024), row),
            pl.BlockSpec((tm, 1024), lambda i: (i, 5)), pl.BlockSpec((tm, 1024), lambda i: (i, 6)),
            pl.BlockSpec((tm, 1024), row),
            pl.BlockSpec(wa.shape, c2), pl.BlockSpec(wm.shape, c2), pl.BlockSpec(wo.shape, c2),
            pl.BlockSpec(n2.shape, c2), pl.BlockSpec(wrh.shape, c2), pl.BlockSpec(wrl.shape, c2),
            pl.BlockSpec(br.shape, c2),
        ],
        out_specs=[pl.BlockSpec((tm, 1024), row), pl.BlockSpec((tm, 1024), row), pl.BlockSpec((tm, LANES), row)],
        out_shape=[jax.ShapeDtypeStruct((n, 1024), F32), jax.ShapeDtypeStruct((n, 1024), BF16),
                   jax.ShapeDtypeStruct((n, LANES), F32)],
        compiler_params=pltpu.CompilerParams(dimension_semantics=("arbitrary",), vmem_limit_bytes=VMEM_LIMIT),
        name="finish",
    )(attn, ssm, y2d, y2d, x2d, wa, wm, wo, n2, wrh, wrl, br)


def _moe_kernel(hn_ref, comb_ref, h_ref, wg_ref, wu_ref, wd_ref, o_ref):
    e = pl.program_id(1)

    @pl.when(e == 0)
    def _():
        o_ref[...] = h_ref[...]

    comb = comb_ref[...]
    lane = lax.broadcasted_iota(jnp.int32, comb.shape, 1)
    c = jnp.sum(jnp.where(lane == e + N_EXPERT_GROUPS, comb, 0.0), axis=-1, keepdims=True)
    x = hn_ref[...]
    a = _nn(x, wg_ref[0])
    u = _nn(x, wu_ref[0])
    act = a * _sigmoid(a) * u * c
    o_ref[...] += _nn(act.astype(BF16), wd_ref[0])


def _moe(hn, comb, h, wg, wu, wd, tm):
    n = hn.shape[0]
    row = lambda i, e: (i, 0)
    return pl.pallas_call(
        _moe_kernel,
        grid=(n // tm, N_EXPERTS),
        in_specs=[
            pl.BlockSpec((tm, 1024), row), pl.BlockSpec((tm, LANES), row), pl.BlockSpec((tm, 1024), row),
            pl.BlockSpec((1, D_MODEL, D_EXPERT), lambda i, e: (e, 0, 0)),
            pl.BlockSpec((1, D_MODEL, D_EXPERT), lambda i, e: (e, 0, 0)),
            pl.BlockSpec((1, D_EXPERT, D_MODEL), lambda i, e: (e, 0, 0)),
        ],
        out_specs=pl.BlockSpec((tm, 1024), row),
        out_shape=jax.ShapeDtypeStruct((n, 1024), F32),
        compiler_params=pltpu.CompilerParams(
            dimension_semantics=("arbitrary", "arbitrary"), vmem_limit_bytes=VMEM_LIMIT),
        name="moe",
    )(hn, comb, h, wg, wu, wd)


def _pick_tile(n, pref):
    t = min(n, pref)
    while n % t:
        t //= 2
    return t


def _lane_pad(v, offset):
    out = jnp.zeros((1, LANES), F32)
    return out.at[0, offset:offset + v.shape[0]].set(v.astype(F32))


def _block_one_hot(nkeys):
    key = np.arange(nkeys)[:, None] // SEL_BLOCK
    return jnp.asarray(key == np.arange(LANES)[None, :], dtype=BF16)


def _cover_matrix(nc):
    n = np.arange(nc)[:, None]
    j = np.arange(LANES)[None, :]
    cstart = n * CMP_STRIDE
    cend = cstart + CMP_BLOCK - 1
    return jnp.asarray((cstart < (j + 1) * SEL_BLOCK) & (cend >= j * SEL_BLOCK), dtype=BF16)


def kernel(x_prompt, x_sample, cache_kv, cache_win_kv, state_conv, state_ssm, page_table, norm1, w_in, q_norm, k_norm, cmp_pe, cmp_w1, cmp_w2, conv_w, conv_b, dt_bias, a_log, d_skip, ssm_norm, w_branch_attn, w_branch_ssm, w_out, norm2, w_router_group, b_router_group, w_router_expert, b_router_expert, w_exp_gate, w_exp_up, w_exp_down):
    assert cache_kv.shape[0] == 1, "single layer"
    bp, sp, _ = x_prompt.shape
    bs, ss, _ = x_sample.shape
    npg = page_table.shape[1]
    past = npg * PAGE_SIZE
    assert sp % SSD_CHUNK == 0 and sp % PAGE_SIZE == 0 and sp >= WINDOW
    assert past % SEL_BLOCK == 0 and ss <= 8 and cache_win_kv.shape[2] == WINDOW

    wi = w_in[0]
    zcol = lambda k: jnp.zeros((D_MODEL, k), F32)
    w_perm = jnp.concatenate([
        wi[:, 0:1024], wi[:, 1024:1792], wi[:, 1792:1840], wi[:, 4912:4928], zcol(64 + 128),
        wi[:, 2864:4912], wi[:, 1840:2864], wi[:, 4928:6976]], axis=1).astype(BF16)
    n1 = norm1[0][None, :]
    qg = jnp.tile(q_norm[0], 2)[None, :]
    kg = jnp.stack([jnp.tile(k_norm[0, 1], 2), jnp.tile(k_norm[0, 2], 2)])
    kn0 = jnp.tile(k_norm[0, 0], 2)[None, :]
    half = CMP_STRIDE * HEAD_DIM
    w1 = cmp_w1[0]
    wcat = jnp.concatenate([w1[:, :half], w1[:, half:]], axis=2).astype(BF16)
    w1b = w1.astype(BF16)
    pe8 = jnp.broadcast_to(cmp_pe[0].reshape(2, 1, CMP_BLOCK * HEAD_DIM), (2, 8, CMP_BLOCK * HEAD_DIM))
    w2 = cmp_w2[0]
    zw = jnp.zeros_like(w2)
    w2lo = jnp.concatenate([w2, zw], axis=2).astype(BF16)
    w2hi = jnp.concatenate([zw, w2], axis=2).astype(BF16)
    slopes = jnp.broadcast_to(jnp.exp2(-8.0 * jnp.arange(1, N_HEADS + 1, dtype=F32) / N_HEADS)[:, None], (N_HEADS, LANES))
    cw, cb = conv_w[0], conv_b[0][None, :]
    dtb = _lane_pad(dt_bias[0], MISC_GATE)
    alog = _lane_pad(a_log[0], MISC_GATE)
    dsk = jnp.repeat(d_skip[0], SSM_HEAD_DIM)[None, :]
    gn = ssm_norm[0][None, :]
    tri = jnp.asarray(np.tril(np.ones((SSD_CHUNK, SSD_CHUNK))), dtype=BF16)
    wa, wm, wo = w_branch_attn[0].astype(BF16), w_branch_ssm[0].astype(BF16), w_out[0].astype(BF16)
    n2 = norm2[0][None, :]
    wr = jnp.concatenate([w_router_group[0], w_router_expert[0],
                          jnp.zeros((D_MODEL, LANES - N_EXPERT_GROUPS - N_EXPERTS), F32)], axis=1)
    wrh = wr.astype(BF16)
    wrl = (wr - wrh.astype(F32)).astype(BF16)
    br = _lane_pad(jnp.concatenate([b_router_group[0], b_router_expert[0]]), 0)
    wg, wu, wd = w_exp_gate[0].astype(BF16), w_exp_up[0].astype(BF16), w_exp_down[0].astype(BF16)

    def trunk(x, nsa_fn, conv_state, h0, ssd_tr):
        b, t, _ = x.shape
        n = b * t
        x2d = x.reshape(n, D_MODEL)
        y2d = _inproj(x2d, n1, w_perm, qg, kg, _pick_tile(n, 1024))
        y3 = y2d.reshape(b, t, Y_COLS)
        attn, extra = nsa_fn(y2d, y3)
        ssm, conv_o, h_o = _ssd(y3, conv_state, h0, cw, cb, dtb, alog, dsk, gn, tri, SSD_CHUNK, ssd_tr)
        h, hn, comb = _finish(attn.reshape(n, 1024), ssm.reshape(n, 1024), y2d, x2d, wa, wm, wo, n2, wrh, wrl, br,
                              _pick_tile(n, 512))
        y = _moe(hn, comb, h, wg, wu, wd, _pick_tile(n, 1024))
        kv_rows = y2d[:, 1024:1536].reshape(1, b, t, 4, KV_HEADS, HEAD_DIM)
        return y.reshape(b, t, D_MODEL), y3, kv_rows, conv_o[None], h_o[None], extra

    def nsa_p(y2d, y3):
        ppt = jnp.arange(bp * (sp // PAGE_SIZE), dtype=jnp.int32).reshape(bp, sp // PAGE_SIZE)
        kc, vc = _compress(y2d.reshape(-1, PAGE_SIZE, Y_COLS), ppt, 1024, wcat, w1b, pe8, w2lo, w2hi, kn0)
        return _nsa_prompt(y3, kc, vc, _block_one_hot(sp), _cover_matrix(kc.shape[1]), slopes, 128, 128), None

    y_p, y3_p, kv_p, conv_p, ssm_p, _ = trunk(
        x_prompt, nsa_p, jnp.zeros((bp, CONV_W - 1, CONV_DIM), F32),
        jnp.zeros((bp, SSM_HEADS, SSM_HEAD_DIM, D_STATE), F32), SSD_CHUNK)
    win_p = y3_p[:, sp - WINDOW:, 1536:1792].reshape(1, bp, WINDOW, 2, KV_HEADS, HEAD_DIM)

    pages = cache_kv[0].reshape(-1, PAGE_SIZE, 4 * KV_HEADS * HEAD_DIM)
    wbuf = cache_win_kv[0].reshape(bs, WINDOW, 2 * KV_HEADS * HEAD_DIM)

    def nsa_s(y2d, y3):
        kc, vc = _compress(pages, page_table, 0, wcat, w1b, pe8, w2lo, w2hi, kn0)
        attn, nwin = _nsa_sample(page_table, y3, kc, vc, wbuf, pages, _block_one_hot(past), _cover_matrix(kc.shape[1]),
                                 slopes, 512)
        return attn, nwin

    y_s, _, kv_s, conv_s, ssm_s, nwin = trunk(x_sample, nsa_s, state_conv[0], state_ssm[0], ss)
    win_s = nwin.reshape(1, bs, WINDOW, 2, KV_HEADS, HEAD_DIM)

    return (y_p, y_s, kv_p, win_p, conv_p, ssm_p, kv_s, win_s, conv_s, ssm_s)
```

```python
import functools

import jax
import jax.numpy as jnp
import numpy as np
from jax import lax
from jax.experimental import pallas as pl
from jax.experimental.pallas import tpu as pltpu

F32 = jnp.float32
BF16 = jnp.bfloat16

D_MODEL = 1024
N_HEADS = 16
HEAD_DIM = 64
KV_HEADS = 2
GROUP = N_HEADS // KV_HEADS
CMP_BLOCK = 32
CMP_STRIDE = 16
CMP_HIDDEN = 2 * HEAD_DIM
SEL_BLOCK = 64
N_SELECT = 16
N_LOCAL = 2
WINDOW = 512
PAGE_SIZE = 128
D_INNER = D_MODEL
SSM_HEAD_DIM = 64
SSM_HEADS = D_INNER // SSM_HEAD_DIM
SSM_GROUPS = 4
D_STATE = 128
CONV_W = 4
CONV_DIM = D_INNER + 2 * SSM_GROUPS * D_STATE
SSD_CHUNK = 128
N_EXPERT_GROUPS = 4
EXPERTS_PER_GROUP = 8
N_EXPERTS = N_EXPERT_GROUPS * EXPERTS_PER_GROUP
D_EXPERT = 512
EPS = 1e-6

LANES = 128
HALF = LANES // 2
NEG = -1e30
Y_COLS = 7 * 1024
MISC_GATE = 3 * N_HEADS
VMEM_LIMIT = 56 * 1024 * 1024


def _nn(a, b):
    return jnp.dot(a, b, preferred_element_type=F32)


def _nt(a, b):
    return lax.dot_general(a, b, (((1,), (1,)), ((), ())), preferred_element_type=F32)


def _sigmoid(x):
    return 1.0 / (1.0 + jnp.exp(-x))


def _lane_lo(shape):
    return lax.broadcasted_iota(jnp.int32, shape, 1) < HALF


def _swap_halves(x):
    return pltpu.roll(x, HALF, axis=1)


def _headnorm128(blk, gain):
    lo = _lane_lo(blk.shape)
    sq = blk * blk
    s_lo = jnp.sum(jnp.where(lo, sq, 0.0), axis=-1, keepdims=True)
    s_hi = jnp.sum(jnp.where(lo, 0.0, sq), axis=-1, keepdims=True)
    inv = jnp.where(lo, lax.rsqrt(s_lo * (1.0 / HALF) + EPS), lax.rsqrt(s_hi * (1.0 / HALF) + EPS))
    return blk * inv * gain


def _split_bf16(x, parts):
    out = []
    r = x
    for _ in range(parts):
        p = r.astype(BF16)
        out.append(p)
        r = r - p.astype(F32)
    return out


def _inproj_kernel(x_ref, n1_ref, w_ref, qg_ref, kg_ref, y_ref, xn_ref):
    j = pl.program_id(1)

    @pl.when(j == 0)
    def _():
        x = x_ref[...]
        inv = lax.rsqrt(jnp.mean(x * x, axis=-1, keepdims=True) + EPS)
        xn_ref[...] = (x * inv * n1_ref[...]).astype(BF16)

    acc = _nn(xn_ref[...], w_ref[...])

    @pl.when(j == 0)
    def _():
        for c in range(D_MODEL // LANES):
            sl = slice(c * LANES, (c + 1) * LANES)
            y_ref[:, sl] = _headnorm128(acc[:, sl], qg_ref[...]) * (HEAD_DIM ** -0.5)

    @pl.when(j == 1)
    def _():
        y_ref[...] = acc
        y_ref[:, 256:384] = _headnorm128(acc[:, 256:384], kg_ref[0:1, :])
        y_ref[:, 512:640] = _headnorm128(acc[:, 512:640], kg_ref[1:2, :])
        misc = acc[:, 768:896]
        lane = lax.broadcasted_iota(jnp.int32, misc.shape, 1)
        y_ref[:, 768:896] = jnp.where(lane < MISC_GATE, _sigmoid(misc), misc)

    @pl.when((j >= 2) & (j <= 4))
    def _():
        y_ref[...] = acc

    @pl.when(j >= 5)
    def _():
        y_ref[...] = _sigmoid(acc)


def _inproj(x2d, n1, w_perm, qg, kg, tm):
    n = x2d.shape[0]
    return pl.pallas_call(
        _inproj_kernel,
        grid=(n // tm, Y_COLS // 1024),
        in_specs=[
            pl.BlockSpec((tm, D_MODEL), lambda i, j: (i, 0)),
            pl.BlockSpec((1, D_MODEL), lambda i, j: (0, 0)),
            pl.BlockSpec((D_MODEL, 1024), lambda i, j: (0, j)),
            pl.BlockSpec((1, LANES), lambda i, j: (0, 0)),
            pl.BlockSpec((2, LANES), lambda i, j: (0, 0)),
        ],
        out_specs=pl.BlockSpec((tm, 1024), lambda i, j: (i, j)),
        out_shape=jax.ShapeDtypeStruct((n, Y_COLS), F32),
        scratch_shapes=[pltpu.VMEM((tm, D_MODEL), BF16)],
        compiler_params=pltpu.CompilerParams(
            dimension_semantics=("arbitrary", "arbitrary"), vmem_limit_bytes=VMEM_LIMIT),
        name="inproj",
    )(x2d, n1, w_perm, qg, kg)


def _token_major_copies(pages_ref, buf_ref, sem, lane0):
    def copies(page, slot, p):
        return [pltpu.make_async_copy(
            pages_ref.at[page, :, pl.ds(lane0 + blk * LANES, LANES)],
            buf_ref.at[slot, blk, pl.ds(pl.multiple_of(p * PAGE_SIZE, PAGE_SIZE), PAGE_SIZE), :],
            sem.at[slot]) for blk in range(buf_ref.shape[1])]
    return copies


def _feature_major_copies(pages_ref, buf_ref, sem, kv_slot0):
    def copies(page, slot, p):
        return [pltpu.make_async_copy(pages_ref.at[page, pl.ds(kv_slot0, 2)], buf_ref.at[slot, p], sem.at[slot])]
    return copies


def _gather_step(copies, pt_ref, npg):
    i = pl.program_id(0)
    n = pl.num_programs(0)
    slot = i % 2

    def start_all(seq, dst_slot):
        def body(p, c):
            for cp in copies(pt_ref[seq, p], dst_slot, p):
                cp.start()
            return c
        lax.fori_loop(0, npg, body, 0)

    @pl.when(i == 0)
    def _():
        start_all(0, 0)

    @pl.when(i + 1 < n)
    def _():
        start_all(i + 1, 1 - slot)

    def wait_body(p, c):
        for cp in copies(0, slot, p):
            cp.wait()
        return c
    lax.fori_loop(0, npg, wait_body, 0)
    return slot


def _gelu_tanh(x):
    return 0.5 * x * (1.0 + jnp.tanh(0.7978845608028654 * (x + 0.044715 * x * x * x)))


def _compress_kernel(pt_ref, pages_ref, wcat_ref, w1_ref, pe_ref, w2lo_ref, w2hi_ref, kn_ref,
                     kc_ref, vc_ref, buf_ref, lt_ref, sem, *rows_scratch, npg, lane0):
    nch = npg * (PAGE_SIZE // CMP_STRIDE)
    if rows_scratch:
        rows_ref, = rows_scratch
        slot = _gather_step(_feature_major_copies(pages_ref, buf_ref, sem, 0), pt_ref, npg)

        def to_rows(p, c):
            off = pl.multiple_of(p * PAGE_SIZE, PAGE_SIZE)
            for s in range(2):
                rows_ref[s, pl.ds(off, PAGE_SIZE), :] = buf_ref[slot, p, s].reshape(LANES, PAGE_SIZE).T
            return c
        lax.fori_loop(0, npg, to_rows, 0, unroll=min(8, npg))
        rows = lambda s, j: rows_ref[s, pl.ds(j, nch, stride=CMP_STRIDE), :]
    else:
        slot = _gather_step(_token_major_copies(pages_ref, buf_ref, sem, lane0), pt_ref, npg)
        rows = lambda s, j: buf_ref[slot, s, pl.ds(j, nch, stride=CMP_STRIDE), :]
    lo = _lane_lo((nch, LANES))
    lt_ref[pl.ds(nch, 8), :] = jnp.zeros((8, 4 * LANES), F32)
    for s in range(2):
        ze, zo = [], []
        for jp in range(CMP_STRIDE // 2):
            xa = rows(s, 2 * jp)
            xb = rows(s, 2 * jp + 1)
            ze.append(jnp.where(lo, xa, _swap_halves(xb)).astype(BF16))
            zo.append(jnp.where(lo, _swap_halves(xa), xb).astype(BF16))
        z = jnp.concatenate([jnp.concatenate(ze, axis=1), jnp.concatenate(zo, axis=1)], axis=0)
        lt = _nn(z, wcat_ref[s])
        lt_ref[pl.ds(0, nch), pl.ds(0, 2 * LANES)] = lt[:nch]
        lt_ref[pl.ds(0, nch), pl.ds(2 * LANES, 2 * LANES)] = lt[nch:]
        pb = _nn(pe_ref[s].astype(BF16), w1_ref[s])[0:1, :]
        hid = []
        for g in range(KV_HEADS):
            lead = lt_ref[pl.ds(0, nch), pl.ds(g * 2 * LANES, LANES)]
            trail = lt_ref[pl.ds(1, nch), pl.ds(g * 2 * LANES + LANES, LANES)]
            hid.append(_gelu_tanh(lead + trail + pb).astype(BF16))
        out = _nn(hid[0], w2lo_ref[s]) + _nn(hid[1], w2hi_ref[s])
        if s == 0:
            kc_ref[0] = _headnorm128(out, kn_ref[...])
        else:
            vc_ref[0] = out


def _compress(pages, page_table, lane0, wcat, w1b, pe8, w2lo, w2hi, kn0):
    nseq, npg = page_table.shape
    nch = npg * (PAGE_SIZE // CMP_STRIDE)
    kern = functools.partial(_compress_kernel, npg=npg, lane0=lane0)
    const3 = lambda i, pt: (0, 0, 0)
    if pages.ndim == 5:
        gather_scratch = [pltpu.VMEM((2, npg, 2, KV_HEADS, HEAD_DIM, PAGE_SIZE), F32)]
        rows_scratch = [pltpu.VMEM((2, npg * PAGE_SIZE, LANES), F32)]
    else:
        gather_scratch = [pltpu.VMEM((2, 2, npg * PAGE_SIZE, LANES), F32)]
        rows_scratch = []
    return pl.pallas_call(
        kern,
        grid_spec=pltpu.PrefetchScalarGridSpec(
            num_scalar_prefetch=1,
            grid=(nseq,),
            in_specs=[
                pl.BlockSpec(memory_space=pl.ANY),
                pl.BlockSpec(wcat.shape, const3),
                pl.BlockSpec(w1b.shape, const3),
                pl.BlockSpec(pe8.shape, const3),
                pl.BlockSpec(w2lo.shape, const3),
                pl.BlockSpec(w2hi.shape, const3),
                pl.BlockSpec((1, LANES), lambda i, pt: (0, 0)),
            ],
            out_specs=[
                pl.BlockSpec((1, nch, LANES), lambda i, pt: (i, 0, 0)),
                pl.BlockSpec((1, nch, LANES), lambda i, pt: (i, 0, 0)),
            ],
            scratch_shapes=gather_scratch + [
                pltpu.VMEM((nch + 8, 4 * LANES), F32),
                pltpu.SemaphoreType.DMA((2,)),
            ] + rows_scratch,
        ),
        out_shape=[jax.ShapeDtypeStruct((nseq, nch, LANES), F32)] * 2,
        compiler_params=pltpu.CompilerParams(
            dimension_semantics=("arbitrary",), vmem_limit_bytes=VMEM_LIMIT),
        name="compress",
    )(page_table, pages, wcat, w1b, pe8, w2lo, w2hi, kn0)


def _head_rows(q, tq):
    lo = _lane_lo((tq, LANES))
    blocks = []
    for h in range(N_HEADS):
        g, j, odd = h // GROUP, h // 2, h % 2
        qp = q[:, j * LANES:(j + 1) * LANES]
        src = qp if odd == g else _swap_halves(qp)
        blocks.append(jnp.where(lo if g == 0 else ~lo, src, 0.0))
    return blocks


def _col16(pieces):
    return jnp.concatenate(pieces, axis=0)


def _cmp_and_select(qbd, kc, vc, cover, slope_col, qpos_col, qpos2, base, tq, nc_valid, nb_lanes, n_top, new_block):
    nc = kc.shape[0]
    n_row = lax.broadcasted_iota(jnp.int32, (1, nc), 1)
    cend = n_row * CMP_STRIDE + (CMP_BLOCK - 1)
    s = _nt(qbd, kc) + slope_col * (cend - base).astype(F32)
    valid = (cend <= qpos_col) & (n_row < nc_valid)
    s = jnp.where(valid, s, NEG)
    m = jnp.max(s, axis=-1, keepdims=True)
    p = jnp.where(valid, jnp.exp(s - m), 0.0)
    pc = p / jnp.maximum(jnp.sum(p, axis=-1, keepdims=True), 1e-30)
    o_c = _nn(pc.astype(BF16), vc)
    ps = []
    for g in range(KV_HEADS):
        acc = pc[(g * GROUP) * tq:(g * GROUP + 1) * tq]
        for r in range(1, GROUP):
            acc = acc + pc[(g * GROUP + r) * tq:(g * GROUP + r + 1) * tq]
        ps.append(acc)
    ps = jnp.concatenate(ps, axis=0)
    hi, lo = _split_bf16(ps, 2)
    imp = _nn(hi, cover) + _nn(lo, cover)
    jb = lax.broadcasted_iota(jnp.int32, imp.shape, 1)
    back = qpos2 // SEL_BLOCK - jb
    forced = (jb == 0) | ((back >= 0) & (back < N_LOCAL))
    impa = jnp.where(forced, jnp.inf, jnp.where(back < 0, -jnp.inf, imp))
    rank = jnp.zeros(imp.shape, jnp.int32)
    for k in range(nb_lanes):
        ck = impa[:, k:k + 1]
        beats = (ck > impa) | ((ck == impa) & (jb > k))
        rank = rank + beats.astype(jnp.int32)
    if new_block:
        rank = rank + jnp.where(impa == jnp.inf, 0, 1)
    selneg = jnp.where(rank < n_top, 0.0, NEG)
    return o_c, selneg


def _flash_t(s, bias, mask, m, l, acc, pv):
    if bias is not None:
        s = s + bias
    if mask is not None:
        s = jnp.where(mask, s, NEG)
    m_new = jnp.maximum(m, jnp.max(s, axis=0, keepdims=True))
    alpha = jnp.exp(m - m_new)
    p = jnp.exp(s - m_new)
    return m_new, alpha * l + jnp.sum(p, axis=0, keepdims=True), alpha * acc + pv(p.astype(BF16))


def _flash_t_out(m, l, acc):
    return jnp.where(m > 0.5 * NEG, acc / jnp.maximum(l, 1e-30), 0.0)


def _flash_t_init(nq):
    return jnp.full((1, nq), NEG, F32), jnp.zeros((1, nq), F32), jnp.zeros((LANES, nq), F32)


def _merge_heads(o_c, o_s, o_w, gates, tq):
    lo = _lane_lo((tq, LANES))
    per_head = []
    for h in range(N_HEADS):
        rows = slice(h * tq, (h + 1) * tq)
        per_head.append(gates[:, 3 * h:3 * h + 1] * o_c[rows] + gates[:, 3 * h + 1:3 * h + 2] * o_s[rows]
                        + gates[:, 3 * h + 2:3 * h + 3] * o_w[rows])
    out = []
    for j in range(N_HEADS // 2):
        ev, od = per_head[2 * j], per_head[2 * j + 1]
        if j // (GROUP // 2) == 0:
            out.append(jnp.where(lo, ev, _swap_halves(od)))
        else:
            out.append(jnp.where(lo, _swap_halves(ev), od))
    return out


def _slope_col(slopes_ref, tq):
    return _col16([jnp.broadcast_to(slopes_ref[h:h + 1, 0:1], (tq, 1)) for h in range(N_HEADS)])


def _nsa_prompt_kernel(q_ref, kv_ref, win_ref, misc_ref, kc_ref, vc_ref, et_ref, cover_ref, slopes_ref,
                       o_ref, ka_s, vt_s, kw_s, vwt_s, qa_s, m_s, l_s, acc_s, *, tq, tk, seq):
    qt = pl.program_id(1)
    start = qt * tq
    wchunks = WINDOW // LANES

    @pl.when(qt == 0)
    def _():
        ka_s[:, 0:LANES] = kv_ref[0, :, 256:384].astype(BF16)
        ka_s[:, LANES:] = et_ref[...]
        kw_s[0:WINDOW, :] = jnp.zeros((WINDOW, LANES), BF16)
        kw_s[WINDOW:, :] = win_ref[0, :, 0:128].astype(BF16)
        for c in range(wchunks):
            vwt_s[c] = jnp.zeros((LANES, LANES), BF16)
        for c in range(seq // LANES):
            rows = slice(c * LANES, (c + 1) * LANES)
            vt_s[c // (tk // LANES), :, (c % (tk // LANES)) * LANES:(c % (tk // LANES) + 1) * LANES] = (
                kv_ref[0, rows, 384:512].T.astype(BF16))
            vwt_s[wchunks + c] = win_ref[0, rows, 128:256].T.astype(BF16)

    qbd = _col16(_head_rows(q_ref[0], tq)).astype(BF16)
    slope_col = _slope_col(slopes_ref, tq)
    tpos = start + lax.broadcasted_iota(jnp.int32, (tq, 1), 0)
    qpos_col = _col16([tpos] * N_HEADS)
    qpos2 = _col16([tpos] * KV_HEADS)
    nb = -(-seq // SEL_BLOCK)
    o_c, selneg = _cmp_and_select(
        qbd, kc_ref[0].astype(BF16), vc_ref[0].astype(BF16), cover_ref[...], slope_col, qpos_col, qpos2, start, tq,
        seq // CMP_STRIDE - 1, nb, min(N_SELECT, nb), False)
    selrep = _col16([selneg[(h // GROUP) * tq:(h // GROUP + 1) * tq] for h in range(N_HEADS)]).astype(BF16)
    qaug = jnp.concatenate([qbd, selrep], axis=1)
    qa_s[...] = qaug
    nq = N_HEADS * tq

    def head_rows(o_t):
        return _col16([o_t[:, h * tq:(h + 1) * tq].T for h in range(N_HEADS)])

    def biased(s, krel, mask):
        return jnp.concatenate([
            jnp.where(mask, s[:, h * tq:(h + 1) * tq] + slopes_ref[h:h + 1, 0:1] * krel, NEG)
            for h in range(N_HEADS)], axis=1)

    tpos_row = start + lax.broadcasted_iota(jnp.int32, (1, tq), 1)

    m_s[...], l_s[...], acc_s[...] = _flash_t_init(nq)

    def sel_body(kt, c):
        off = pl.multiple_of(kt * tk, tk)
        vt = vt_s[kt]
        kpos = off + lax.broadcasted_iota(jnp.int32, (tk, tq), 0)
        s = biased(_nt(ka_s[pl.ds(off, tk), :], qa_s[...]), (kpos - start).astype(F32), kpos <= tpos_row)
        m_s[...], l_s[...], acc_s[...] = _flash_t(s, None, None, m_s[...], l_s[...], acc_s[...], lambda p: _nn(vt, p))
        return c

    lax.fori_loop(0, (start + tq + tk - 1) // tk, sel_body, 0)
    o_s = head_rows(_flash_t_out(m_s[...], l_s[...], acc_s[...]))

    nwin = WINDOW + tq
    kposw = (start - WINDOW) + lax.broadcasted_iota(jnp.int32, (nwin, tq), 0)
    dw = tpos_row - kposw
    c0 = start // LANES

    def pv_w(p):
        return sum(_nn(vwt_s[c0 + i], p[i * LANES:(i + 1) * LANES]) for i in range(nwin // LANES))

    sw = biased(_nt(kw_s[pl.ds(pl.multiple_of(start, tq), nwin), :], qa_s[:, 0:LANES]), (kposw - start).astype(F32),
                (dw >= 0) & (dw < WINDOW) & (kposw >= 0))
    o_w = head_rows(_flash_t_out(*_flash_t(sw, None, None, *_flash_t_init(nq), pv_w)))

    out = _merge_heads(o_c, o_s, o_w, misc_ref[0], tq)
    for j in range(N_HEADS // 2):
        o_ref[0, :, j * LANES:(j + 1) * LANES] = out[j]


def _nsa_prompt(y3, kc, vc, et, cover, slopes, tq, tk):
    assert tq == LANES, "one head per 128-column block"
    b, seq, _ = y3.shape
    nc = kc.shape[1]
    nq = N_HEADS * tq
    kern = functools.partial(_nsa_prompt_kernel, tq=tq, tk=tk, seq=seq)
    return pl.pallas_call(
        kern,
        grid=(b, seq // tq),
        in_specs=[
            pl.BlockSpec((1, tq, 1024), lambda i, j: (i, j, 0)),
            pl.BlockSpec((1, seq, 512), lambda i, j: (i, 0, 2)),
            pl.BlockSpec((1, seq, 256), lambda i, j: (i, 0, 6)),
            pl.BlockSpec((1, tq, LANES), lambda i, j: (i, j, 14)),
            pl.BlockSpec((1, nc, LANES), lambda i, j: (i, 0, 0)),
            pl.BlockSpec((1, nc, LANES), lambda i, j: (i, 0, 0)),
            pl.BlockSpec(et.shape, lambda i, j: (0, 0)),
            pl.BlockSpec(cover.shape, lambda i, j: (0, 0)),
            pl.BlockSpec(slopes.shape, lambda i, j: (0, 0)),
        ],
        out_specs=pl.BlockSpec((1, tq, 1024), lambda i, j: (i, j, 0)),
        out_shape=jax.ShapeDtypeStruct((b, seq, 1024), F32),
        scratch_shapes=[
            pltpu.VMEM((seq, 2 * LANES), BF16),
            pltpu.VMEM((seq // tk, LANES, tk), BF16),
            pltpu.VMEM((seq + WINDOW, LANES), BF16),
            pltpu.VMEM(((seq + WINDOW) // LANES, LANES, LANES), BF16),
            pltpu.VMEM((nq, 2 * LANES), BF16),
            pltpu.VMEM((1, nq), F32), pltpu.VMEM((1, nq), F32),
            pltpu.VMEM((LANES, nq), F32),
        ],
        compiler_params=pltpu.CompilerParams(
            dimension_semantics=("arbitrary", "arbitrary"), vmem_limit_bytes=VMEM_LIMIT),
        name="nsa_prompt",
    )(y3, y3, y3, y3, kc, vc, et, cover, slopes)


def _pad_rows(x, rows):
    return jnp.concatenate([x, jnp.zeros((rows - x.shape[0], x.shape[1]), x.dtype)], axis=0)


def _nsa_sample_kernel(pt_ref, q_ref, new_ref, kc_ref, vc_ref, wbuf_ref, pages_ref, et_ref, cover_ref, slopes_ref,
                       srow_ref, o_ref, nwin_ref, buf_ref, ka_s, sem, *, ts, tk, npg, past):
    slot = _gather_step(_feature_major_copies(pages_ref, buf_ref, sem, 2), pt_ref, npg)
    nq = N_HEADS * ts

    @pl.when(pl.program_id(0) == 0)
    def _():
        ka_s[:, LANES:] = et_ref[...]

    def to_rows(p, c):
        off = pl.multiple_of(p * PAGE_SIZE, PAGE_SIZE)
        ka_s[pl.ds(off, PAGE_SIZE), 0:LANES] = buf_ref[slot, p, 0].reshape(LANES, PAGE_SIZE).T.astype(BF16)
        return c
    lax.fori_loop(0, npg, to_rows, 0, unroll=min(8, npg))

    qbd = _col16(_head_rows(q_ref[0], ts)).astype(BF16)
    slope_col = _slope_col(slopes_ref, ts)
    tpos = past + lax.broadcasted_iota(jnp.int32, (ts, 1), 0)
    qpos_col = _col16([tpos] * N_HEADS)
    qpos2 = _col16([tpos] * KV_HEADS)
    nb_past = past // SEL_BLOCK
    o_c, selneg = _cmp_and_select(
        qbd, kc_ref[0].astype(BF16), vc_ref[0].astype(BF16), cover_ref[...], slope_col, qpos_col, qpos2, past, ts,
        past // CMP_STRIDE - 1, nb_past, N_SELECT, True)
    selrep = _col16([selneg[(h // GROUP) * ts:(h // GROUP + 1) * ts] for h in range(N_HEADS)]).astype(BF16)
    qaug = jnp.concatenate([qbd, selrep], axis=1)

    srow = srow_ref[...]
    tcol = lax.rem(lax.broadcasted_iota(jnp.int32, (1, nq), 1), ts)
    init = _flash_t_init(nq)

    new = new_ref[0]
    ipos = lax.broadcasted_iota(jnp.int32, (LANES, nq), 0)
    nmask = (ipos <= tcol) & (ipos < ts)
    nbias = srow * ipos.astype(F32)

    def new_tile(lane0, m, l, acc):
        kn = _pad_rows(new[:, lane0:lane0 + LANES], LANES).astype(BF16)
        vnt = _pad_rows(new[:, lane0 + LANES:lane0 + 2 * LANES], LANES).T.astype(BF16)
        return _flash_t(_nt(kn, qbd), nbias, nmask, m, l, acc, lambda p: _nn(vnt, p))

    ppt = tk // PAGE_SIZE

    def sel_body(kt, carry):
        off = pl.multiple_of(kt * tk, tk)
        krel = (off - past + lax.broadcasted_iota(jnp.int32, (tk, nq), 0)).astype(F32)

        def pv(p):
            return sum(_nn(buf_ref[slot, kt * ppt + i, 1].reshape(LANES, PAGE_SIZE).astype(BF16),
                           p[i * PAGE_SIZE:(i + 1) * PAGE_SIZE]) for i in range(ppt))
        return _flash_t(_nt(ka_s[pl.ds(off, tk), :], qaug), srow * krel, None, *carry, pv)

    m, l, acc = lax.fori_loop(0, past // tk, sel_body, init)
    o_s = _flash_t_out(*new_tile(256, m, l, acc)).T

    w0 = wbuf_ref.shape[1]
    wpos = (past - w0) + lax.broadcasted_iota(jnp.int32, (w0, nq), 0)
    dw = (past + tcol) - wpos
    maskw = (dw >= 0) & (dw < WINDOW) & (wpos >= 0)

    def pv_w(p):
        return sum(_nn(wbuf_ref[0, i * LANES:(i + 1) * LANES, 128:256].T.astype(BF16), p[i * LANES:(i + 1) * LANES])
                   for i in range(w0 // LANES))

    m, l, acc = _flash_t(_nt(wbuf_ref[0, :, 0:128].astype(BF16), qbd), srow * (wpos - past).astype(F32), maskw,
                         *init, pv_w)
    o_w = _flash_t_out(*new_tile(512, m, l, acc)).T

    out = _merge_heads(o_c, o_s, o_w, new[:, 768:896], ts)
    for j in range(N_HEADS // 2):
        o_ref[0, :, j * LANES:(j + 1) * LANES] = out[j]
    nwin_ref[0, pl.ds(0, w0 - ts), :] = wbuf_ref[0, pl.ds(ts, w0 - ts), :]
    nwin_ref[0, pl.ds(w0 - ts, ts), :] = new[:, 512:768]


def _nsa_sample(page_table, ys3, kc, vc, wbuf, pages, et, cover, slopes, srow, tk):
    nseq, ts, _ = ys3.shape
    npg = page_table.shape[1]
    past = npg * PAGE_SIZE
    nc = kc.shape[1]
    w0 = wbuf.shape[1]
    kern = functools.partial(_nsa_sample_kernel, ts=ts, tk=tk, npg=npg, past=past)
    return pl.pallas_call(
        kern,
        grid_spec=pltpu.PrefetchScalarGridSpec(
            num_scalar_prefetch=1,
            grid=(nseq,),
            in_specs=[
                pl.BlockSpec((1, ts, 1024), lambda i, pt: (i, 0, 0)),
                pl.BlockSpec((1, ts, 1024), lambda i, pt: (i, 0, 1)),
                pl.BlockSpec((1, nc, LANES), lambda i, pt: (i, 0, 0)),
                pl.BlockSpec((1, nc, LANES), lambda i, pt: (i, 0, 0)),
                pl.BlockSpec((1, w0, 256), lambda i, pt: (i, 0, 0)),
                pl.BlockSpec(memory_space=pl.ANY),
                pl.BlockSpec(et.shape, lambda i, pt: (0, 0)),
                pl.BlockSpec(cover.shape, lambda i, pt: (0, 0)),
                pl.BlockSpec(slopes.shape, lambda i, pt: (0, 0)),
                pl.BlockSpec(srow.shape, lambda i, pt: (0, 0)),
            ],
            out_specs=[
                pl.BlockSpec((1, ts, 1024), lambda i, pt: (i, 0, 0)),
                pl.BlockSpec((1, w0, 256), lambda i, pt: (i, 0, 0)),
            ],
            scratch_shapes=[
                pltpu.VMEM((2, npg, 2, KV_HEADS, HEAD_DIM, PAGE_SIZE), F32),
                pltpu.VMEM((past, 2 * LANES), BF16),
                pltpu.SemaphoreType.DMA((2,)),
            ],
        ),
        out_shape=[jax.ShapeDtypeStruct((nseq, ts, 1024), F32), jax.ShapeDtypeStruct((nseq, w0, 256), F32)],
        compiler_params=pltpu.CompilerParams(
            dimension_semantics=("arbitrary",), vmem_limit_bytes=VMEM_LIMIT),
        name="nsa_sample",
    )(page_table, ys3, ys3, kc, vc, wbuf, pages, et, cover, slopes, srow)


def _softplus(x):
    return jnp.maximum(x, 0.0) + jnp.log(1.0 + jnp.exp(-jnp.abs(x)))


def _ssd_kernel(xbc_ref, z_ref, misc_ref, cs_ref, h0_ref, cw_ref, cb_ref, dtb_ref, alog_ref, dsk_ref, gn_ref, tri_ref,
                y_ref, cso_ref, ho_ref, xf_s, ht_s, *, L, TR):
    c = pl.program_id(1)
    nchunks = pl.num_programs(1)
    npair = SSM_HEADS // 2

    @pl.when(c == 0)
    def _():
        xf_s[0:8, :] = jnp.zeros((8, CONV_DIM), F32)
        xf_s[8 - (CONV_W - 1):8, :] = cs_ref[0]
        for j in range(npair):
            ht_s[j] = jnp.concatenate([h0_ref[0, 2 * j], h0_ref[0, 2 * j + 1]], axis=0).T

    @pl.when(c > 0)
    def _():
        xf_s[0:8, :] = xf_s[L:L + 8, :]

    x = xbc_ref[0]
    z = z_ref[0]
    misc = misc_ref[0]
    if TR < L:
        x, z, misc = _pad_rows(x, L), _pad_rows(z, L), _pad_rows(misc, L)
    xf_s[8:8 + L, :] = x
    conv = cb_ref[...]
    for k in range(CONV_W):
        conv = conv + cw_ref[k:k + 1, :] * xf_s[pl.ds(8 - (CONV_W - 1) + k, L), :]
    u = conv * _sigmoid(conv)
    xs = u[:, 0:D_INNER]
    bm = u[:, D_INNER:D_INNER + SSM_GROUPS * D_STATE]
    cm = u[:, D_INNER + SSM_GROUPS * D_STATE:]

    dt = _softplus(misc + dtb_ref[...])
    if TR < L:
        dt = jnp.where(lax.broadcasted_iota(jnp.int32, dt.shape, 0) < TR, dt, 0.0)
    ad = dt * (-jnp.exp(alog_ref[...]))
    tri = tri_ref[...]
    acum = sum(_nn(tri, part) for part in _split_bf16(ad, 3))
    acum_t = acum.T
    tot = acum[L - 1:L, :]
    rows = lax.broadcasted_iota(jnp.int32, (L, L), 0)
    cols = lax.broadcasted_iota(jnp.int32, (L, L), 1)
    causal = rows >= cols
    lo = _lane_lo((L, LANES))
    lo1 = _lane_lo((1, LANES))
    d0 = MISC_GATE

    ys = []
    for g in range(SSM_GROUPS):
        bc = bm[:, g * D_STATE:(g + 1) * D_STATE]
        ccb = cm[:, g * D_STATE:(g + 1) * D_STATE].astype(BF16)
        cb = _nt(ccb, bc.astype(BF16))
        bct = bc.T.astype(BF16)
        for jj in range(2):
            j = 2 * g + jj
            he, ho = 2 * j, 2 * j + 1
            a_e, a_o = acum[:, d0 + he:d0 + he + 1], acum[:, d0 + ho:d0 + ho + 1]
            r_e, r_o = acum_t[d0 + he:d0 + he + 1, :], acum_t[d0 + ho:d0 + ho + 1, :]
            t_e, t_o = tot[:, d0 + he:d0 + he + 1], tot[:, d0 + ho:d0 + ho + 1]
            lm_e = jnp.where(causal, jnp.exp(jnp.where(causal, a_e - r_e, 0.0)), 0.0)
            lm_o = jnp.where(causal, jnp.exp(jnp.where(causal, a_o - r_o, 0.0)), 0.0)
            xs_p = xs[:, j * LANES:(j + 1) * LANES]
            xg_p = xs_p * jnp.where(lo, dt[:, d0 + he:d0 + he + 1], dt[:, d0 + ho:d0 + ho + 1])
            y_p = (_nn((cb * lm_e).astype(BF16), jnp.where(lo, xg_p, 0.0).astype(BF16))
                   + _nn((cb * lm_o).astype(BF16), jnp.where(lo, 0.0, xg_p).astype(BF16)))
            h_prev = ht_s[j]
            y_p = y_p + jnp.where(lo, jnp.exp(a_e), jnp.exp(a_o)) * _nn(ccb, h_prev.astype(BF16))
            dec = jnp.where(lo, jnp.exp(t_e - a_e), jnp.exp(t_o - a_o))
            st = _nn(bct, (xg_p * dec).astype(BF16))
            ht_s[j] = h_prev * jnp.where(lo1, jnp.exp(t_e), jnp.exp(t_o)) + st
            ys.append(y_p + dsk_ref[:, j * LANES:(j + 1) * LANES] * xs_p)

    gw = D_INNER // SSM_GROUPS
    zg = z * _sigmoid(z)
    for gi in range(SSM_GROUPS):
        blk = jnp.concatenate(ys[gi * 2:gi * 2 + 2], axis=1) * zg[:, gi * gw:(gi + 1) * gw]
        inv = lax.rsqrt(jnp.mean(blk * blk, axis=-1, keepdims=True) + EPS)
        y_ref[0, :, gi * gw:(gi + 1) * gw] = (blk * inv * gn_ref[:, gi * gw:(gi + 1) * gw])[:TR]

    @pl.when(c == nchunks - 1)
    def _():
        cso_ref[0] = xf_s[pl.ds(8 + TR - (CONV_W - 1), CONV_W - 1), :]
        for j in range(npair):
            t = ht_s[j].T
            ho_ref[0, 2 * j] = t[0:SSM_HEAD_DIM]
            ho_ref[0, 2 * j + 1] = t[SSM_HEAD_DIM:]


def _ssd(y3, conv_state, h0, cw, cb, dtb, alog, dsk, gn, tri, L, TR):
    b, t, _ = y3.shape
    nchunks = t // TR
    kern = functools.partial(_ssd_kernel, L=L, TR=TR)
    c2 = lambda i, j: (0, 0)
    return pl.pallas_call(
        kern,
        grid=(b, nchunks),
        in_specs=[
            pl.BlockSpec((1, TR, CONV_DIM), lambda i, j: (i, j, 1)),
            pl.BlockSpec((1, TR, D_INNER), lambda i, j: (i, j, 4)),
            pl.BlockSpec((1, TR, LANES), lambda i, j: (i, j, 14)),
            pl.BlockSpec((1, CONV_W - 1, CONV_DIM), lambda i, j: (i, 0, 0)),
            pl.BlockSpec((1, SSM_HEADS, SSM_HEAD_DIM, D_STATE), lambda i, j: (i, 0, 0, 0)),
            pl.BlockSpec(cw.shape, c2), pl.BlockSpec(cb.shape, c2), pl.BlockSpec(dtb.shape, c2),
            pl.BlockSpec(alog.shape, c2), pl.BlockSpec(dsk.shape, c2), pl.BlockSpec(gn.shape, c2),
            pl.BlockSpec(tri.shape, c2),
        ],
        out_specs=[
            pl.BlockSpec((1, TR, D_INNER), lambda i, j: (i, j, 0)),
            pl.BlockSpec((1, CONV_W - 1, CONV_DIM), lambda i, j: (i, 0, 0)),
            pl.BlockSpec((1, SSM_HEADS, SSM_HEAD_DIM, D_STATE), lambda i, j: (i, 0, 0, 0)),
        ],
        out_shape=[
            jax.ShapeDtypeStruct((b, t, D_INNER), F32),
            jax.ShapeDtypeStruct((b, CONV_W - 1, CONV_DIM), F32),
            jax.ShapeDtypeStruct((b, SSM_HEADS, SSM_HEAD_DIM, D_STATE), F32),
        ],
        scratch_shapes=[pltpu.VMEM((L + 8, CONV_DIM), F32), pltpu.VMEM((SSM_HEADS // 2, D_STATE, LANES), F32)],
        compiler_params=pltpu.CompilerParams(
            dimension_semantics=("arbitrary", "arbitrary"), vmem_limit_bytes=VMEM_LIMIT),
        name="ssd",
    )(y3, y3, y3, conv_state, h0, cw, cb, dtb, alog, dsk, gn, tri)


def _finish_kernel(attn_ref, ssm_ref, ga_ref, gm_ref, x_ref, wa_ref, wm_ref, wo_ref, n2_ref, wrh_ref, wrl_ref, br_ref,
                   h_ref, hn_ref, comb_ref):
    a = _nn(attn_ref[...].astype(BF16), wa_ref[...])
    s = _nn(ssm_ref[...].astype(BF16), wm_ref[...])
    mixed = ga_ref[...] * a + gm_ref[...] * s
    h = x_ref[...] + _nn(mixed.astype(BF16), wo_ref[...])
    h_ref[...] = h
    hn = h * lax.rsqrt(jnp.mean(h * h, axis=-1, keepdims=True) + EPS) * n2_ref[...]
    hn_ref[...] = hn.astype(BF16)
    hh, hl = _split_bf16(hn, 2)
    logits = _nn(hh, wrh_ref[...]) + _nn(hh, wrl_ref[...]) + _nn(hl, wrh_ref[...]) + br_ref[...]

    lane = lax.broadcasted_iota(jnp.int32, logits.shape, 1)
    big = jnp.int32(4 * LANES)
    in_g = lane < N_EXPERT_GROUPS
    gl = jnp.where(in_g, logits, -jnp.inf)
    gmax = jnp.max(gl, axis=-1, keepdims=True)
    gidx = jnp.min(jnp.where(gl == gmax, lane, big), axis=-1, keepdims=True)
    gw = 1.0 / jnp.sum(jnp.where(in_g, jnp.exp(logits - gmax), 0.0), axis=-1, keepdims=True)
    e0 = N_EXPERT_GROUPS + EXPERTS_PER_GROUP * gidx
    el = jnp.where((lane >= e0) & (lane < e0 + EXPERTS_PER_GROUP), logits, -jnp.inf)
    v1 = jnp.max(el, axis=-1, keepdims=True)
    i1 = jnp.min(jnp.where(el == v1, lane, big), axis=-1, keepdims=True)
    el2 = jnp.where(lane == i1, -jnp.inf, el)
    v2 = jnp.max(el2, axis=-1, keepdims=True)
    i2 = jnp.min(jnp.where(el2 == v2, lane, big), axis=-1, keepdims=True)
    e = jnp.exp(v2 - v1)
    w1 = gw / (1.0 + e)
    comb_ref[...] = jnp.where(lane == i1, w1, jnp.where(lane == i2, w1 * e, 0.0))


def _finish(attn, ssm, y2d, x2d, wa, wm, wo, n2, wrh, wrl, br, tm):
    n = x2d.shape[0]
    c2 = lambda i: (0, 0)
    row = lambda i: (i, 0)
    return pl.pallas_call(
        _finish_kernel,
        grid=(n // tm,),
        in_specs=[
            pl.BlockSpec((tm, 1024), row), pl.BlockSpec((tm, 1024), row),
            pl.BlockSpec((tm, 1024), lambda i: (i, 5)), pl.BlockSpec((tm, 1024), lambda i: (i, 6)),
            pl.BlockSpec((tm, 1024), row),
            pl.BlockSpec(wa.shape, c2), pl.BlockSpec(wm.shape, c2), pl.BlockSpec(wo.shape, c2),
            pl.BlockSpec(n2.shape, c2), pl.BlockSpec(wrh.shape, c2), pl.BlockSpec(wrl.shape, c2),
            pl.BlockSpec(br.shape, c2),
        ],
        out_specs=[pl.BlockSpec((tm, 1024), row), pl.BlockSpec((tm, 1024), row), pl.BlockSpec((tm, LANES), row)],
        out_shape=[jax.ShapeDtypeStruct((n, 1024), F32), jax.ShapeDtypeStruct((n, 1024), BF16),
                   jax.ShapeDtypeStruct((n, LANES), F32)],
        compiler_params=pltpu.CompilerParams(dimension_semantics=("arbitrary",), vmem_limit_bytes=VMEM_LIMIT),
        name="finish",
    )(attn, ssm, y2d, y2d, x2d, wa, wm, wo, n2, wrh, wrl, br)


def _moe_kernel(hn_ref, comb_ref, h_ref, wg_ref, wu_ref, wd_ref, o_ref):
    e = pl.program_id(1)

    @pl.when(e == 0)
    def _():
        o_ref[...] = h_ref[...]

    comb = comb_ref[...]
    lane = lax.broadcasted_iota(jnp.int32, comb.shape, 1)
    c = jnp.sum(jnp.where(lane == e + N_EXPERT_GROUPS, comb, 0.0), axis=-1, keepdims=True)
    x = hn_ref[...]
    a = _nn(x, wg_ref[0])
    u = _nn(x, wu_ref[0])
    act = a * _sigmoid(a) * u * c
    o_ref[...] += _nn(act.astype(BF16), wd_ref[0])


def _moe(hn, comb, h, wg, wu, wd, tm):
    n = hn.shape[0]
    row = lambda i, e: (i, 0)
    return pl.pallas_call(
        _moe_kernel,
        grid=(n // tm, N_EXPERTS),
        in_specs=[
            pl.BlockSpec((tm, 1024), row), pl.BlockSpec((tm, LANES), row), pl.BlockSpec((tm, 1024), row),
            pl.BlockSpec((1, D_MODEL, D_EXPERT), lambda i, e: (e, 0, 0)),
            pl.BlockSpec((1, D_MODEL, D_EXPERT), lambda i, e: (e, 0, 0)),
            pl.BlockSpec((1, D_EXPERT, D_MODEL), lambda i, e: (e, 0, 0)),
        ],
        out_specs=pl.BlockSpec((tm, 1024), row),
        out_shape=jax.ShapeDtypeStruct((n, 1024), F32),
        compiler_params=pltpu.CompilerParams(
            dimension_semantics=("arbitrary", "arbitrary"), vmem_limit_bytes=VMEM_LIMIT),
        name="moe",
    )(hn, comb, h, wg, wu, wd)


def _pick_tile(n, pref):
    t = min(n, pref)
    while n % t:
        t //= 2
    return t


def _lane_pad(v, offset):
    out = jnp.zeros((1, LANES), F32)
    return out.at[0, offset:offset + v.shape[0]].set(v.astype(F32))


def _block_one_hot(nkeys):
    key = np.arange(nkeys)[:, None] // SEL_BLOCK
    return jnp.asarray(key == np.arange(LANES)[None, :], dtype=BF16)


def _cover_matrix(nc):
    n = np.arange(nc)[:, None]
    j = np.arange(LANES)[None, :]
    cstart = n * CMP_STRIDE
    cend = cstart + CMP_BLOCK - 1
    return jnp.asarray((cstart < (j + 1) * SEL_BLOCK) & (cend >= j * SEL_BLOCK), dtype=BF16)


def kernel(x_prompt, x_sample, cache_kv, cache_win_kv, state_conv, state_ssm, page_table, norm1, w_in, q_norm, k_norm, cmp_pe, cmp_w1, cmp_w2, conv_w, conv_b, dt_bias, a_log, d_skip, ssm_norm, w_branch_attn, w_branch_ssm, w_out, norm2, w_router_group, b_router_group, w_router_expert, b_router_expert, w_exp_gate, w_exp_up, w_exp_down):
    assert cache_kv.shape[0] == 1, "single layer"
    bp, sp, _ = x_prompt.shape
    bs, ss, _ = x_sample.shape
    npg = page_table.shape[1]
    past = npg * PAGE_SIZE
    assert sp % SSD_CHUNK == 0 and sp % PAGE_SIZE == 0 and sp >= WINDOW
    assert past % SEL_BLOCK == 0 and ss <= 8 and cache_win_kv.shape[2] == WINDOW

    wi = w_in[0]
    zcol = lambda k: jnp.zeros((D_MODEL, k), F32)
    w_perm = jnp.concatenate([
        wi[:, 0:1024], wi[:, 1024:1792], wi[:, 1792:1840], wi[:, 4912:4928], zcol(64 + 128),
        wi[:, 2864:4912], wi[:, 1840:2864], wi[:, 4928:6976]], axis=1).astype(BF16)
    n1 = norm1[0][None, :]
    qg = jnp.tile(q_norm[0], 2)[None, :]
    kg = jnp.stack([jnp.tile(k_norm[0, 1], 2), jnp.tile(k_norm[0, 2], 2)])
    kn0 = jnp.tile(k_norm[0, 0], 2)[None, :]
    half = CMP_STRIDE * HEAD_DIM
    w1 = cmp_w1[0]
    wcat = jnp.concatenate([w1[:, :half], w1[:, half:]], axis=2).astype(BF16)
    w1b = w1.astype(BF16)
    pe8 = jnp.broadcast_to(cmp_pe[0].reshape(2, 1, CMP_BLOCK * HEAD_DIM), (2, 8, CMP_BLOCK * HEAD_DIM))
    w2 = cmp_w2[0]
    zw = jnp.zeros_like(w2)
    w2lo = jnp.concatenate([w2, zw], axis=2).astype(BF16)
    w2hi = jnp.concatenate([zw, w2], axis=2).astype(BF16)
    slopes = jnp.broadcast_to(jnp.exp2(-8.0 * jnp.arange(1, N_HEADS + 1, dtype=F32) / N_HEADS)[:, None], (N_HEADS, LANES))
    cw, cb = conv_w[0], conv_b[0][None, :]
    dtb = _lane_pad(dt_bias[0], MISC_GATE)
    alog = _lane_pad(a_log[0], MISC_GATE)
    dsk = jnp.repeat(d_skip[0], SSM_HEAD_DIM)[None, :]
    gn = ssm_norm[0][None, :]
    tri = jnp.asarray(np.tril(np.ones((SSD_CHUNK, SSD_CHUNK))), dtype=BF16)
    wa, wm, wo = w_branch_attn[0].astype(BF16), w_branch_ssm[0].astype(BF16), w_out[0].astype(BF16)
    n2 = norm2[0][None, :]
    wr = jnp.concatenate([w_router_group[0], w_router_expert[0],
                          jnp.zeros((D_MODEL, LANES - N_EXPERT_GROUPS - N_EXPERTS), F32)], axis=1)
    wrh = wr.astype(BF16)
    wrl = (wr - wrh.astype(F32)).astype(BF16)
    br = _lane_pad(jnp.concatenate([b_router_group[0], b_router_expert[0]]), 0)
    wg, wu, wd = w_exp_gate[0].astype(BF16), w_exp_up[0].astype(BF16), w_exp_down[0].astype(BF16)

    def trunk(x, nsa_fn, conv_state, h0, ssd_tr):
        b, t, _ = x.shape
        n = b * t
        x2d = x.reshape(n, D_MODEL)
        y2d = _inproj(x2d, n1, w_perm, qg, kg, _pick_tile(n, 1024))
        y3 = y2d.reshape(b, t, Y_COLS)
        attn, extra = nsa_fn(y2d, y3)
        ssm, conv_o, h_o = _ssd(y3, conv_state, h0, cw, cb, dtb, alog, dsk, gn, tri, SSD_CHUNK, ssd_tr)
        h, hn, comb = _finish(attn.reshape(n, 1024), ssm.reshape(n, 1024), y2d, x2d, wa, wm, wo, n2, wrh, wrl, br,
                              _pick_tile(n, 512))
        y = _moe(hn, comb, h, wg, wu, wd, _pick_tile(n, 1024))
        kv_rows = y2d[:, 1024:1536].reshape(1, b, t, 4, KV_HEADS, HEAD_DIM)
        return y.reshape(b, t, D_MODEL), y3, kv_rows, conv_o[None], h_o[None], extra

    def nsa_p(y2d, y3):
        ppt = jnp.arange(bp * (sp // PAGE_SIZE), dtype=jnp.int32).reshape(bp, sp // PAGE_SIZE)
        kc, vc = _compress(y2d.reshape(-1, PAGE_SIZE, Y_COLS), ppt, 1024, wcat, w1b, pe8, w2lo, w2hi, kn0)
        return _nsa_prompt(y3, kc, vc, _block_one_hot(sp), _cover_matrix(kc.shape[1]), slopes, LANES, 512), None

    y_p, y3_p, kv_p, conv_p, ssm_p, _ = trunk(
        x_prompt, nsa_p, jnp.zeros((bp, CONV_W - 1, CONV_DIM), F32),
        jnp.zeros((bp, SSM_HEADS, SSM_HEAD_DIM, D_STATE), F32), SSD_CHUNK)
    win_p = y3_p[:, sp - WINDOW:, 1536:1792].reshape(1, bp, WINDOW, 2, KV_HEADS, HEAD_DIM)

    pages = jnp.transpose(cache_kv[0], (0, 2, 3, 4, 1))
    wbuf = cache_win_kv[0].reshape(bs, WINDOW, 2 * KV_HEADS * HEAD_DIM)
    srow = jnp.repeat(slopes[:, 0], ss)[None, :]

    def nsa_s(y2d, y3):
        kc, vc = _compress(pages, page_table, 0, wcat, w1b, pe8, w2lo, w2hi, kn0)
        attn, nwin = _nsa_sample(page_table, y3, kc, vc, wbuf, pages, _block_one_hot(past), _cover_matrix(kc.shape[1]),
                                 slopes, srow, min(2048, past))
        return attn, nwin

    y_s, _, kv_s, conv_s, ssm_s, nwin = trunk(x_sample, nsa_s, state_conv[0], state_ssm[0], ss)
    win_s = nwin.reshape(1, bs, WINDOW, 2, KV_HEADS, HEAD_DIM)

    return (y_p, y_s, kv_p, win_p, conv_p, ssm_p, kv_s, win_s, conv_s, ssm_s)
```
